```python
import math
import jax, jax.numpy as jnp
from jax import lax
import numpy as np

D_MODEL = 1024
BATCH = 2
SEQ = 16384
DEPTH = 2

F32 = jnp.float32
NEG = -1e30
LN_EPS = 1e-5
RMS_EPS = 1e-6
DN_ALPHA = (2.0 * DEPTH) ** 0.25
DN_BETA = (8.0 * DEPTH) ** -0.25

A_HEADS = 4
A_DK = 128
A_DV = 128
A_CHUNK = 64
A_KW = A_HEADS * A_DK
A_VW = A_HEADS * A_DV

B_WIDTH = 512
B_GROUP = 16
B_GROUPS = B_WIDTH // B_GROUP
B_STATE = 64
DT_MIN = 1e-3
DT_MAX = 1e-1

HEAD_DIM = 64
C_HEADS = 8
C_KV = 2
WINDOW = 128
D_HEADS = 8
D_KV = 2
MOBA_BLOCK = 256
MOBA_TOPK = 3
MOBA_QCHUNK = 64
C_QW = C_HEADS * HEAD_DIM
C_KVW = C_KV * HEAD_DIM
D_QW = D_HEADS * HEAD_DIM
D_KVW = D_KV * HEAD_DIM

EV_IN = 2 * A_KW + 2 * A_VW + B_WIDTH
EV_MIX = A_VW + B_WIDTH
OD_IN = C_QW + 2 * C_KVW + D_QW + 2 * D_KVW
OD_MIX = C_QW + D_QW

N_GROUPS = 4
EXP_PER_GROUP = 4
N_EXPERTS = N_GROUPS * EXP_PER_GROUP
EXP_HIDDEN = 256
EXP_TOPK = 2

kernel_name = 'hybrid_hgrn2_s5_swa_moba_hmoe'


def split_cols(p, widths):
    outs, start = [], 0
    for w in widths:
        outs.append(p[..., start:start + w])
        start += w
    return outs


def layer_norm(x, g, b):
    xf = x.astype(F32)
    mu = jnp.mean(xf, axis=-1, keepdims=True)
    var = jnp.mean(jnp.square(xf - mu), axis=-1, keepdims=True)
    return ((xf - mu) * lax.rsqrt(var + LN_EPS) * g.astype(F32) + b.astype(F32)).astype(x.dtype)


def hgrn2_mixer(q, f_logit, inp, gate, lower_bound, norm_g):
    Bsz, L, _ = q.shape
    n = L // A_CHUNK

    def heads(t, d):
        return t.astype(F32).reshape(Bsz, n, A_CHUNK, A_HEADS, d).transpose(0, 3, 1, 2, 4)

    lb = lower_bound.astype(F32)
    f = lb + (1.0 - lb) * jax.nn.sigmoid(f_logit.astype(F32))
    qh = heads(jax.nn.silu(q.astype(F32)), A_DK)
    kh = heads(1.0 - f, A_DK)
    lf = heads(jnp.log(f), A_DK)
    vh = heads(inp, A_DV)
    b = jnp.cumsum(lf, axis=3)
    b_last = b[:, :, :, -1:, :]
    q_dec = qh * jnp.exp(b)
    k_inv = kh * jnp.exp(-b)
    k_tail = kh * jnp.exp(b_last - b)
    causal = jnp.tril(jnp.ones((A_CHUNK, A_CHUNK), dtype=bool))
    att = jnp.where(causal, jnp.einsum('bhnck,bhnsk->bhncs', q_dec, k_inv), 0.0)
    o_intra = jnp.einsum('bhncs,bhnsv->bhncv', att, vh)
    d_state = jnp.einsum('bhnsk,bhnsv->bhnkv', k_tail, vh)
    chunk_decay = jnp.exp(b_last[:, :, :, 0, :])

    def step(S, xs):
        dec, ds = xs
        return dec[..., None] * S + ds, S

    S0 = jnp.zeros((Bsz, A_HEADS, A_DK, A_DV), F32)
    _, S_prev = lax.scan(step, S0, (jnp.moveaxis(chunk_decay, 2, 0), jnp.moveaxis(d_state, 2, 0)))
    S_prev = jnp.moveaxis(S_prev, 0, 2)
    o = o_intra + jnp.einsum('bhnck,bhnkv->bhncv', q_dec, S_prev)
    o = o * lax.rsqrt(jnp.mean(o * o, axis=-1, keepdims=True) + RMS_EPS)
    o = o.transpose(0, 2, 3, 1, 4).reshape(Bsz, L, A_VW)
    return (o * norm_g.astype(F32) * jax.nn.silu(gate.astype(F32))).astype(q.dtype)


def s5_mixer(u, a_re, a_im, log_dt, b_re, b_im, c_re, c_im, d_skip, w_glu):
    Bsz, L, _ = u.shape
    uf = u.astype(F32)
    ug = uf.reshape(Bsz, L, B_GROUPS, B_GROUP)
    dt = jnp.exp(log_dt.astype(F32))[:, None]
    ar, ai = a_re.astype(F32), a_im.astype(F32)
    mag = jnp.exp(dt * ar)
    abar_re, abar_im = mag * jnp.cos(dt * ai), mag * jnp.sin(dt * ai)
    den = ar * ar + ai * ai
    xr, xi = abar_re - 1.0, abar_im
    fr = (xr * ar + xi * ai) / den
    fi = (xi * ar - xr * ai) / den
    br, bi = b_re.astype(F32), b_im.astype(F32)
    bbar_re = fr[..., None] * br - fi[..., None] * bi
    bbar_im = fr[..., None] * bi + fi[..., None] * br
    drive_re = jnp.einsum('blgn,gpn->blgp', ug, bbar_re)
    drive_im = jnp.einsum('blgn,gpn->blgp', ug, bbar_im)
    a_full_re = jnp.broadcast_to(abar_re, drive_re.shape)
    a_full_im = jnp.broadcast_to(abar_im, drive_re.shape)

    def combine(e1, e2):
        a1r, a1i, b1r, b1i = e1
        a2r, a2i, b2r, b2i = e2
        return (a1r * a2r - a1i * a2i, a1r * a2i + a1i * a2r,
                a2r * b1r - a2i * b1i + b2r, a2r * b1i + a2i * b1r + b2i)

    _, _, h_re, h_im = lax.associative_scan(combine, (a_full_re, a_full_im, drive_re, drive_im), axis=1)
    y = (jnp.einsum('blgp,gnp->blgn', h_re, c_re.astype(F32))
         - jnp.einsum('blgp,gnp->blgn', h_im, c_im.astype(F32)))
    y = jax.nn.gelu(y.reshape(Bsz, L, B_WIDTH) + d_skip.astype(F32) * uf)
    z = jnp.einsum('blc,cf->blf', y, w_glu.astype(F32))
    return (z[..., :B_WIDTH] * jax.nn.sigmoid(z[..., B_WIDTH:])).astype(u.dtype)


def sliding_window_attention(q, k, v, sinks):
    Bsz, L, _ = q.shape
    nb = L // WINDOW
    G = C_HEADS // C_KV
    scale = HEAD_DIM ** -0.5
    qb = q.astype(F32).reshape(Bsz, nb, WINDOW, C_KV, G, HEAD_DIM) * scale
    kb = k.astype(F32).reshape(Bsz, nb, WINDOW, C_KV, HEAD_DIM)
    vb = v.astype(F32).reshape(Bsz, nb, WINDOW, C_KV, HEAD_DIM)
    prev = lambda t: jnp.concatenate([jnp.zeros_like(t[:, :1]), t[:, :-1]], axis=1)
    kk = jnp.concatenate([prev(kb), kb], axis=2)
    vv = jnp.concatenate([prev(vb), vb], axis=2)
    s = jnp.einsum('bnqhgd,bnkhd->bnhgqk', qb, kk)
    qpos = jnp.arange(WINDOW)[:, None] + WINDOW
    kpos = jnp.arange(2 * WINDOW)[None, :]
    band = (kpos <= qpos) & (qpos - kpos < WINDOW)
    first = (jnp.arange(nb) == 0)[:, None, None] & (kpos < WINDOW)[None]
    valid = band[None] & ~first
    s = jnp.where(valid[None, :, None, None], s, NEG)
    sink = sinks.astype(F32).reshape(C_KV, G)[None, None, :, :, None, None]
    m = jnp.maximum(jnp.max(s, axis=-1, keepdims=True), sink)
    p = jnp.exp(s - m)
    denom = jnp.sum(p, axis=-1, keepdims=True) + jnp.exp(sink - m)
    o = jnp.einsum('bnhgqk,bnkhd->bnqhgd', p / denom, vv)
    return o.reshape(Bsz, L, C_QW).astype(q.dtype)


def moba_attention(q, k, v):
    Bsz, L, _ = q.shape
    Lp = -(-L // MOBA_BLOCK) * MOBA_BLOCK
    nb = Lp // MOBA_BLOCK
    G = D_HEADS // D_KV
    scale = HEAD_DIM ** -0.5
    pad = ((0, 0), (0, Lp - L), (0, 0))
    qh = (jnp.pad(q.astype(F32), pad) * scale).reshape(Bsz, Lp, D_HEADS, HEAD_DIM).transpose(0, 2, 1, 3)
    kh = jnp.pad(k.astype(F32), pad).reshape(Bsz, Lp, D_KV, HEAD_DIM).transpose(0, 2, 1, 3)
    vh = jnp.pad(v.astype(F32), pad).reshape(Bsz, Lp, D_KV, HEAD_DIM).transpose(0, 2, 1, 3)
    kblk = kh.reshape(Bsz, D_KV, nb, MOBA_BLOCK, HEAD_DIM)
    vblk = vh.reshape(Bsz, D_KV, nb, MOBA_BLOCK, HEAD_DIM)
    qblk = qh.reshape(Bsz, D_KV, G, nb, MOBA_BLOCK, HEAD_DIM)
    causal = jnp.tril(jnp.ones((MOBA_BLOCK, MOBA_BLOCK), dtype=bool))
    s_own = jnp.where(causal, jnp.einsum('bhgnqd,bhnkd->bhgnqk', qblk, kblk), NEG)
    m_own = jnp.max(s_own, axis=-1)
    p_own = jnp.exp(s_own - m_own[..., None])
    l_own = jnp.sum(p_own, axis=-1).reshape(Bsz, D_HEADS, Lp)
    acc_own = jnp.einsum('bhgnqk,bhnkd->bhgnqd', p_own, vblk).reshape(Bsz, D_HEADS, Lp, HEAD_DIM)
    m_own = m_own.reshape(Bsz, D_HEADS, Lp)
    k_mean = jnp.mean(kblk, axis=3)
    cur_blk = jnp.arange(Lp) // MOBA_BLOCK
    past = jnp.arange(nb)[None, :] < cur_blk[:, None]
    gate = jnp.einsum('bhgtd,bhnd->bhgtn', qh.reshape(Bsz, D_KV, G, Lp, HEAD_DIM), k_mean)
    gate = jnp.where(past, gate, NEG).reshape(Bsz, D_HEADS, Lp, nb)
    n_sel = min(MOBA_TOPK, nb)
    _, sel = lax.top_k(gate, n_sel)
    sel_valid = sel < cur_blk[None, None, :, None]
    nc = Lp // MOBA_QCHUNK

    def chunks(t):
        return jnp.moveaxis(t.reshape(Bsz, D_HEADS, nc, MOBA_QCHUNK, *t.shape[3:]), 2, 0)

    bi = jnp.arange(Bsz)[:, None, None, None]
    hi = (jnp.arange(D_HEADS) // G)[None, :, None, None]

    def sel_chunk(args):
        qc, ic, vc, mo, lo, ao = args
        kg = kblk[bi, hi, ic]
        vg = vblk[bi, hi, ic]
        s = jnp.where(vc[..., None, None] if vc.ndim == 3 else vc[..., None],
                      jnp.einsum('bhqd,bhqsnd->bhqsn', qc, kg), NEG)
        m = jnp.maximum(mo, jnp.max(s, axis=(-2, -1)))
        p = jnp.exp(s - m[..., None, None])
        corr = jnp.exp(mo - m)
        l = lo * corr + jnp.sum(p, axis=(-2, -1))
        acc = ao * corr[..., None] + jnp.einsum('bhqsn,bhqsnd->bhqd', p, vg)
        return acc / l[..., None]

    out = lax.map(sel_chunk, (chunks(qh), chunks(sel), chunks(sel_valid),
                              chunks(m_own), chunks(l_own), chunks(acc_own)))
    out = jnp.moveaxis(out, 0, 2).reshape(Bsz, D_HEADS, Lp, HEAD_DIM).transpose(0, 2, 1, 3)
    return out.reshape(Bsz, Lp, D_QW)[:, :L].astype(q.dtype)


def hier_moe(x, w_group, b_group, w_expert, b_expert, w_gate_up, w_down):
    Bsz, L, D = x.shape
    xt = x.reshape(Bsz * L, D)
    g_prob = jax.nn.softmax((xt @ w_group).astype(F32) + b_group.astype(F32), axis=-1)
    g_top, g_idx = lax.top_k(g_prob, 1)
    e_logits = ((xt @ w_expert).astype(F32) + b_expert.astype(F32)).reshape(-1, N_GROUPS, EXP_PER_GROUP)
    e_in = jnp.take_along_axis(e_logits, g_idx[:, :, None], axis=1)[:, 0]
    e_top, e_idx = lax.top_k(e_in, EXP_TOPK)
    w_sel = jax.nn.softmax(e_top, axis=-1) * g_top
    eid = g_idx * EXP_PER_GROUP + e_idx
    gates = jnp.einsum('tk,tke->te', w_sel, jax.nn.one_hot(eid, N_EXPERTS, dtype=F32))
    y = jnp.zeros((Bsz * L, D), F32)
    for g in range(N_GROUPS):
        sl = slice(g * EXP_PER_GROUP, (g + 1) * EXP_PER_GROUP)
        hu = jnp.einsum('td,edf->tef', xt, w_gate_up[sl])
        h = jax.nn.silu(hu[..., :EXP_HIDDEN]) * hu[..., EXP_HIDDEN:]
        h = h * gates[:, sl, None].astype(h.dtype)
        y = y + jnp.einsum('tef,efd->td', h, w_down[sl])
    return y.reshape(Bsz, L, D).astype(x.dtype)


def setup_inputs(seed: int = 0) -> dict:
    key = jax.random.key(seed)
    ks = iter(jax.random.split(key, 48))
    nrm = lambda shape, s: jax.random.normal(next(ks), shape, F32) * s
    D = D_MODEL
    NEV = (DEPTH + 1) // 2
    NOD = DEPTH // 2
    sd = D ** -0.5
    x = nrm((BATCH, SEQ, D), 1.0)
    hgrn_lb_logits = nrm((DEPTH + 1, A_KW), 0.1)
    ev_w_in = jnp.concatenate([
        nrm((NEV, D, A_KW), sd),
        nrm((NEV, D, A_KW), sd),
        nrm((NEV, D, A_VW), sd * DN_BETA),
        nrm((NEV, D, A_VW), sd),
        nrm((NEV, D, B_WIDTH), sd * DN_BETA),
    ], axis=-1)
    ev_a_norm = 1.0 + nrm((NEV, A_VW), 0.02)
    ev_s5_a_re = -0.5 * jnp.exp(nrm((NEV, B_GROUPS, B_STATE), 0.02))
    ev_s5_a_im = math.pi * jnp.arange(B_STATE, dtype=F32) + nrm((NEV, B_GROUPS, B_STATE), 0.02)
    ev_s5_log_dt = jax.random.uniform(next(ks), (NEV, B_GROUPS), F32,
                                      minval=math.log(DT_MIN), maxval=math.log(DT_MAX))
    ev_s5_b_re = nrm((NEV, B_GROUPS, B_STATE, B_GROUP), (2.0 * B_GROUP) ** -0.5)
    ev_s5_b_im = nrm((NEV, B_GROUPS, B_STATE, B_GROUP), (2.0 * B_GROUP) ** -0.5)
    ev_s5_c_re = nrm((NEV, B_GROUPS, B_GROUP, B_STATE), (2.0 * B_STATE) ** -0.5)
    ev_s5_c_im = nrm((NEV, B_GROUPS, B_GROUP, B_STATE), (2.0 * B_STATE) ** -0.5)
    ev_s5_d = nrm((NEV, B_WIDTH), 1.0)
    ev_s5_w_glu = nrm((NEV, B_WIDTH, 2 * B_WIDTH), B_WIDTH ** -0.5)
    ev_w_out = nrm((NEV, EV_MIX, D), EV_MIX ** -0.5 * DN_BETA)
    od_w_in = jnp.concatenate([
        nrm((NOD, D, C_QW), sd), nrm((NOD, D, C_KVW), sd), nrm((NOD, D, C_KVW), sd * DN_BETA),
        nrm((NOD, D, D_QW), sd), nrm((NOD, D, D_KVW), sd), nrm((NOD, D, D_KVW), sd * DN_BETA),
    ], axis=-1)
    od_sinks = nrm((NOD, C_HEADS), 1.0)
    od_w_out = nrm((NOD, OD_MIX, D), OD_MIX ** -0.5 * DN_BETA)
    ln1_g = 1.0 + nrm((DEPTH, D), 0.02)
    ln1_b = nrm((DEPTH, D), 0.02)
    moe_w_group = nrm((DEPTH, D, N_GROUPS), sd)
    moe_b_group = nrm((DEPTH, N_GROUPS), 0.01)
    moe_w_expert = nrm((DEPTH, D, N_EXPERTS), sd)
    moe_b_expert = nrm((DEPTH, N_EXPERTS), 0.01)
    moe_w_gate_up = nrm((DEPTH, N_EXPERTS, D, 2 * EXP_HIDDEN), sd * DN_BETA)
    moe_w_down = nrm((DEPTH, N_EXPERTS, EXP_HIDDEN, D), EXP_HIDDEN ** -0.5 * DN_BETA)
    ln2_g = 1.0 + nrm((DEPTH, D), 0.02)
    ln2_b = nrm((DEPTH, D), 0.02)
    return {'x': x, 'hgrn_lb_logits': hgrn_lb_logits, 'ev_w_in': ev_w_in, 'ev_a_norm': ev_a_norm,
            'ev_s5_a_re': ev_s5_a_re, 'ev_s5_a_im': ev_s5_a_im, 'ev_s5_log_dt': ev_s5_log_dt,
            'ev_s5_b_re': ev_s5_b_re, 'ev_s5_b_im': ev_s5_b_im, 'ev_s5_c_re': ev_s5_c_re,
            'ev_s5_c_im': ev_s5_c_im, 'ev_s5_d': ev_s5_d, 'ev_s5_w_glu': ev_s5_w_glu,
            'ev_w_out': ev_w_out, 'od_w_in': od_w_in, 'od_sinks': od_sinks, 'od_w_out': od_w_out,
            'ln1_g': ln1_g, 'ln1_b': ln1_b, 'moe_w_group': moe_w_group, 'moe_b_group': moe_b_group,
            'moe_w_expert': moe_w_expert, 'moe_b_expert': moe_b_expert, 'moe_w_gate_up': moe_w_gate_up,
            'moe_w_down': moe_w_down, 'ln2_g': ln2_g, 'ln2_b': ln2_b}


def reference(x, hgrn_lb_logits, ev_w_in, ev_a_norm, ev_s5_a_re, ev_s5_a_im, ev_s5_log_dt,
              ev_s5_b_re, ev_s5_b_im, ev_s5_c_re, ev_s5_c_im, ev_s5_d, ev_s5_w_glu, ev_w_out,
              od_w_in, od_sinks, od_w_out, ln1_g, ln1_b, moe_w_group, moe_b_group, moe_w_expert,
              moe_b_expert, moe_w_gate_up, moe_w_down, ln2_g, ln2_b):
    lower_bounds = jnp.cumsum(jax.nn.softmax(hgrn_lb_logits.astype(F32), axis=0), axis=0)
    for layer in range(DEPTH):
        j = layer // 2
        if layer % 2 == 0:
            proj = jnp.einsum('bld,df->blf', x, ev_w_in[j])
            q_a, f_a, i_a, g_a, u_b = split_cols(proj, [A_KW, A_KW, A_VW, A_VW, B_WIDTH])
            y_a = hgrn2_mixer(q_a, f_a, i_a, g_a, lower_bounds[layer], ev_a_norm[j])
            y_b = s5_mixer(u_b, ev_s5_a_re[j], ev_s5_a_im[j], ev_s5_log_dt[j], ev_s5_b_re[j],
                           ev_s5_b_im[j], ev_s5_c_re[j], ev_s5_c_im[j], ev_s5_d[j], ev_s5_w_glu[j])
            mix = jnp.einsum('blf,fd->bld', jnp.concatenate([y_a, y_b], axis=-1), ev_w_out[j])
        else:
            proj = jnp.einsum('bld,df->blf', x, od_w_in[j])
            q_c, k_c, v_c, q_d, k_d, v_d = split_cols(proj, [C_QW, C_KVW, C_KVW, D_QW, D_KVW, D_KVW])
            y_c = sliding_window_attention(q_c, k_c, v_c, od_sinks[j])
            y_d = moba_attention(q_d, k_d, v_d)
            mix = jnp.einsum('blf,fd->bld', jnp.concatenate([y_c, y_d], axis=-1), od_w_out[j])
        x = layer_norm(DN_ALPHA * x + mix.astype(x.dtype), ln1_g[layer], ln1_b[layer])
        ffn = hier_moe(x, moe_w_group[layer], moe_b_group[layer], moe_w_expert[layer],
                       moe_b_expert[layer], moe_w_gate_up[layer], moe_w_down[layer])
        x = layer_norm(DN_ALPHA * x + ffn, ln2_g[layer], ln2_b[layer])
    return x
```

```python
import functools
import math

import jax
import jax.numpy as jnp
from jax import lax
from jax.experimental import pallas as pl
from jax.experimental.pallas import tpu as pltpu

F32 = jnp.float32
BF16 = jnp.bfloat16
NEG = -1e30
LN_EPS = 1e-5
RMS_EPS = 1e-6

A_HEADS = 4
A_DK = 128
A_DV = 128
A_CHUNK = 64
A_W = A_HEADS * A_DK

B_WIDTH = 512
B_GROUP = 16
B_GROUPS = B_WIDTH // B_GROUP
B_STATE = 64
S5_LC = 16
S5_CW = S5_LC * B_GROUP

HEAD_DIM = 64
C_HEADS = 8
C_KV = 2
WINDOW = 128
D_HEADS = 8
D_KV = 2
MOBA_BLOCK = 256
MOBA_TOPK = 3
GQA = C_HEADS // C_KV

N_GROUPS = 4
EXP_PER_GROUP = 4
N_EXPERTS = 16
EXP_HIDDEN = 256

VMEM_LIMIT_BYTES = 48 * 1024 * 1024


def _cparams(*sem):
    return pltpu.CompilerParams(dimension_semantics=sem, vmem_limit_bytes=VMEM_LIMIT_BYTES)


def _nt_dot(a, b):
    return lax.dot_general(a, b, (((1,), (1,)), ((), ())), preferred_element_type=F32)


def _dot(a, b):
    return jnp.dot(a, b, preferred_element_type=F32)


def _split_bf16(v):
    hi = v.astype(BF16)
    lo = (v - hi.astype(F32)).astype(BF16)
    return hi, lo


def _silu(v):
    return v * jax.nn.sigmoid(v)


def _layer_norm_rows(r, g, b):
    mu = jnp.mean(r, axis=-1, keepdims=True)
    d = r - mu
    var = jnp.mean(d * d, axis=-1, keepdims=True)
    return d * lax.rsqrt(var + LN_EPS) * g + b


def _proj_kernel(x_ref, w_ref, o_ref):
    o_ref[...] = _dot(x_ref[...].astype(BF16), w_ref[...])


def _proj(x2d, w_bf, tm=512):
    t, d = x2d.shape
    n = w_bf.shape[1]
    return pl.pallas_call(
        _proj_kernel,
        grid=(t // tm,),
        in_specs=[pl.BlockSpec((tm, d), lambda i: (i, 0)),
                  pl.BlockSpec((d, n), lambda i: (0, 0))],
        out_specs=pl.BlockSpec((tm, n), lambda i: (i, 0)),
        out_shape=jax.ShapeDtypeStruct((t, n), F32),
        compiler_params=_cparams("parallel"),
        name="in_proj",
    )(x2d, w_bf)


def _mix_ln_kernel(ya_ref, yb_ref, x_ref, w_ref, g_ref, b_ref, o_ref, *, alpha):
    half = ya_ref.shape[1]
    mix = _dot(ya_ref[...].astype(BF16), w_ref[0:half, :]) + _dot(yb_ref[...].astype(BF16), w_ref[half:, :])
    o_ref[...] = _layer_norm_rows(alpha * x_ref[...] + mix, g_ref[...], b_ref[...])


def _mix_ln(ya, yb, x2d, w_out_bf, g, b, alpha, tm=512):
    t, d = x2d.shape
    half = ya.shape[1]
    return pl.pallas_call(
        functools.partial(_mix_ln_kernel, alpha=alpha),
        grid=(t // tm,),
        in_specs=[pl.BlockSpec((tm, half), lambda i: (i, 0)),
                  pl.BlockSpec((tm, half), lambda i: (i, 0)),
                  pl.BlockSpec((tm, d), lambda i: (i, 0)),
                  pl.BlockSpec((2 * half, d), lambda i: (0, 0)),
                  pl.BlockSpec((1, d), lambda i: (0, 0)),
                  pl.BlockSpec((1, d), lambda i: (0, 0))],
        out_specs=pl.BlockSpec((tm, d), lambda i: (i, 0)),
        out_shape=jax.ShapeDtypeStruct((t, d), F32),
        compiler_params=_cparams("parallel"),
        name="mix_ln",
    )(ya, yb, x2d, w_out_bf, g.reshape(1, d), b.reshape(1, d))


def _hgrn2_kernel(proj_ref, lb_ref, ng_ref, tri_ref, o_ref, st_ref, *, tb):
    @pl.when(pl.program_id(1) == 0)
    def _():
        st_ref[...] = jnp.zeros_like(st_ref)

    ql = proj_ref[:, 0:A_W]
    fl = proj_ref[:, A_W:2 * A_W]
    iv = proj_ref[:, 2 * A_W:3 * A_W]
    gt = proj_ref[:, 3 * A_W:4 * A_W]
    lb = lb_ref[...]
    f = lb + (1.0 - lb) * jax.nn.sigmoid(fl)
    k = 1.0 - f
    lf = jnp.log(f)
    lf_hi, lf_lo = _split_bf16(lf)
    tri = tri_ref[...]
    b = _dot(tri, lf_hi) + _dot(tri, lf_lo)
    qd = _silu(ql) * jnp.exp(b)
    ki = k * jnp.exp(-b)
    og = ng_ref[...] * _silu(gt)

    row = lax.broadcasted_iota(jnp.int32, (A_CHUNK, A_CHUNK), 0)
    col = lax.broadcasted_iota(jnp.int32, (A_CHUNK, A_CHUNK), 1)
    causal = col <= row
    v_t = [iv[:, h * A_DV:(h + 1) * A_DV].T for h in range(A_HEADS)]

    for c in range(tb // A_CHUNK):
        r0 = c * A_CHUNK
        b_c = b[r0:r0 + A_CHUNK, :]
        b_last = b[r0 + A_CHUNK - 1:r0 + A_CHUNK, :]
        kt = k[r0:r0 + A_CHUNK, :] * jnp.exp(b_last - b_c)
        dec = jnp.exp(b_last)
        for h in range(A_HEADS):
            ls = slice(h * A_DK, (h + 1) * A_DK)
            qd_h = qd[r0:r0 + A_CHUNK, ls].astype(BF16)
            ki_h = ki[r0:r0 + A_CHUNK, ls].astype(BF16)
            v_h = iv[r0:r0 + A_CHUNK, ls]
            att = jnp.where(causal, _nt_dot(qd_h, ki_h), 0.0)
            s_t = st_ref[h]
            o = _dot(att.astype(BF16), v_h.astype(BF16)) + _nt_dot(qd_h, s_t.astype(BF16))
            st_ref[h] = s_t * dec[:, ls] + _dot(v_t[h][:, r0:r0 + A_CHUNK].astype(BF16), kt[:, ls].astype(BF16))
            o = o * lax.rsqrt(jnp.mean(o * o, axis=-1, keepdims=True) + RMS_EPS)
            o_ref[r0:r0 + A_CHUNK, ls] = o * og[r0:r0 + A_CHUNK, ls]


def _hgrn2(proj, lb, norm_g, tb=256):
    bsz, l, _ = proj.shape
    tb = min(tb, l)
    idx = jnp.arange(tb)
    tri = ((idx[:, None] >= idx[None, :]) & (idx[:, None] // A_CHUNK == idx[None, :] // A_CHUNK)).astype(BF16)
    return pl.pallas_call(
        functools.partial(_hgrn2_kernel, tb=tb),
        grid=(bsz, l // tb),
        in_specs=[pl.BlockSpec((None, tb, 4 * A_W), lambda b, t: (b, t, 0)),
                  pl.BlockSpec((1, A_W), lambda b, t: (0, 0)),
                  pl.BlockSpec((1, A_W), lambda b, t: (0, 0)),
                  pl.BlockSpec((tb, tb), lambda b, t: (0, 0))],
        out_specs=pl.BlockSpec((None, tb, A_W), lambda b, t: (b, t, 0)),
        out_shape=jax.ShapeDtypeStruct((bsz, l, A_W), F32),
        scratch_shapes=[pltpu.VMEM((A_HEADS, A_DV, A_DK), F32)],
        compiler_params=_cparams("parallel", "arbitrary"),
        name="hgrn2",
    )(proj, lb.reshape(1, A_W), norm_g.reshape(1, A_W), tri)


def _s5_weights(a_re, a_im, log_dt, b_re, b_im, c_re, c_im):
    g, p = a_re.shape
    lc = S5_LC
    dt = jnp.exp(log_dt.astype(F32))[:, None]
    ar, ai = a_re.astype(F32), a_im.astype(F32)
    mag = jnp.exp(dt * ar)
    abar_re, abar_im = mag * jnp.cos(dt * ai), mag * jnp.sin(dt * ai)
    den = ar * ar + ai * ai
    xr, xi = abar_re - 1.0, abar_im
    fr = (xr * ar + xi * ai) / den
    fi = (xi * ar - xr * ai) / den
    br, bi = b_re.astype(F32), b_im.astype(F32)
    bb_re = fr[..., None] * br - fi[..., None] * bi
    bb_im = fr[..., None] * bi + fi[..., None] * br
    cr, ci = c_re.astype(F32), c_im.astype(F32)
    tau = jnp.arange(lc + 1, dtype=F32)[:, None, None]
    pmag = jnp.exp(tau * (dt * ar)[None])
    pw_re = pmag * jnp.cos(tau * (dt * ai)[None])
    pw_im = pmag * jnp.sin(tau * (dt * ai)[None])
    hp = lax.Precision.HIGHEST
    ab_re = pw_re[..., None] * bb_re[None] - pw_im[..., None] * bb_im[None]
    ab_im = pw_re[..., None] * bb_im[None] + pw_im[..., None] * bb_re[None]
    kk = (jnp.einsum('gnp,tgpm->tgnm', cr, ab_re, precision=hp)
          - jnp.einsum('gnp,tgpm->tgnm', ci, ab_im, precision=hp))
    s_idx = jnp.arange(lc)[:, None]
    t_idx = jnp.arange(lc)[None, :]
    lag = t_idx - s_idx
    toe = jnp.where((lag >= 0)[:, :, None, None, None], kk[jnp.clip(lag, 0, lc)], 0.0)
    toe = toe.transpose(2, 0, 4, 1, 3).reshape(g, lc * B_GROUP, lc * B_GROUP)
    rev = lc - 1 - jnp.arange(lc)
    wx_re = ab_re[rev].transpose(1, 0, 3, 2).reshape(g, lc * B_GROUP, p)
    wx_im = ab_im[rev].transpose(1, 0, 3, 2).reshape(g, lc * B_GROUP, p)
    pr, pi_ = pw_re[1:], pw_im[1:]
    wy_re = cr[None] * pr[:, :, None, :] - ci[None] * pi_[:, :, None, :]
    wy_im = -(cr[None] * pi_[:, :, None, :] + ci[None] * pr[:, :, None, :])
    wy_re = wy_re.transpose(1, 3, 0, 2).reshape(g, p, lc * B_GROUP)
    wy_im = wy_im.transpose(1, 3, 0, 2).reshape(g, p, lc * B_GROUP)
    return toe, wx_re, wx_im, wy_re, wy_im, pw_re[lc], pw_im[lc]


def _pair_pack_rows(w):
    g, r, c = w.shape
    w = w.reshape(g // 2, 2, r, c)
    z = jnp.zeros((g // 2, r, c), w.dtype)
    top = jnp.concatenate([w[:, 0], z], axis=2)
    bot = jnp.concatenate([z, w[:, 1]], axis=2)
    return jnp.concatenate([top, bot], axis=1)


def _s5_inject_kernel(u_ref, wre_ref, wim_ref, xre_ref, xim_ref):
    lhs = jnp.concatenate([u_ref[0], u_ref[1]], axis=1).astype(BF16)
    xre_ref[...] = _dot(lhs, wre_ref[...])
    xim_ref[...] = _dot(lhs, wim_ref[...])


def _s5_scan_kernel(xre_ref, xim_ref, are_ref, aim_ref, hre_ref, him_ref, sre_ref, sim_ref, *, tcs):
    @pl.when(pl.program_id(0) == 0)
    def _():
        sre_ref[...] = jnp.zeros_like(sre_ref)
        sim_ref[...] = jnp.zeros_like(sim_ref)

    ar = are_ref[...]
    ai = aim_ref[...]

    def body(r, carry):
        hr, hi = carry
        hre_ref[r] = hr
        him_ref[r] = hi
        return ar * hr - ai * hi + xre_ref[r], ar * hi + ai * hr + xim_ref[r]

    hr, hi = lax.fori_loop(0, tcs, body, (sre_ref[...], sim_ref[...]), unroll=8)
    sre_ref[...] = hr
    sim_ref[...] = hi


def _s5_readout_kernel(u_ref, toe_ref, wyre_ref, wyim_ref, hre_ref, him_ref, y_ref):
    hre = hre_ref[...].astype(BF16)
    him = him_ref[...].astype(BF16)
    for r in range(2):
        y_ref[r] = (_dot(u_ref[r].astype(BF16), toe_ref[r])
                    + _dot(hre, wyre_ref[r]) + _dot(him, wyim_ref[r]))


def _s5_core(u, a_re, a_im, log_dt, b_re, b_im, c_re, c_im):
    bsz, l, _ = u.shape
    g, p, lc, cw = B_GROUPS, B_STATE, S5_LC, S5_CW
    ncb = l // lc
    toe, wx_re, wx_im, wy_re, wy_im, a_re_lc, a_im_lc = _s5_weights(a_re, a_im, log_dt, b_re, b_im, c_re, c_im)
    toe = toe.astype(BF16)
    wx_re_p = _pair_pack_rows(wx_re).astype(BF16)
    wx_im_p = _pair_pack_rows(wx_im).astype(BF16)
    zero = jnp.zeros_like(wy_re)
    even = (jnp.arange(g) % 2 == 0)[:, None, None]
    wy_re_p = jnp.where(even, jnp.concatenate([wy_re, zero], 1), jnp.concatenate([zero, wy_re], 1)).astype(BF16)
    wy_im_p = jnp.where(even, jnp.concatenate([wy_im, zero], 1), jnp.concatenate([zero, wy_im], 1)).astype(BF16)

    u2 = u.reshape(bsz, ncb, lc, g, B_GROUP).transpose(3, 0, 1, 2, 4).reshape(g, bsz * ncb, cw)
    tc = min(1024, ncb)
    nct = ncb // tc
    slab = 2 * p
    x_shape = jax.ShapeDtypeStruct((ncb, bsz * g * p), F32)
    x_spec = pl.BlockSpec((tc, slab), lambda j, b, i: (i, b * (g // 2) + j))
    u_spec = pl.BlockSpec((2, tc, cw), lambda j, b, i: (j, b * nct + i, 0))
    xre, xim = pl.pallas_call(
        _s5_inject_kernel,
        grid=(g // 2, bsz, nct),
        in_specs=[u_spec,
                  pl.BlockSpec((None, 2 * cw, slab), lambda j, b, i: (j, 0, 0)),
                  pl.BlockSpec((None, 2 * cw, slab), lambda j, b, i: (j, 0, 0))],
        out_specs=[x_spec, x_spec],
        out_shape=[x_shape, x_shape],
        compiler_params=_cparams("parallel", "parallel", "parallel"),
        name="s5_inject",
    )(u2, wx_re_p, wx_im_p)

    lanes = 512
    rows = bsz * g * p // lanes
    tcs = min(128, ncb)
    a_tile = lambda a: jnp.tile(a.reshape(g * p // lanes, lanes), (bsz, 1))
    seq_spec = pl.BlockSpec((tcs, rows, lanes), lambda i: (i, 0, 0))
    par_spec = pl.BlockSpec((rows, lanes), lambda i: (0, 0))
    seq_shape = jax.ShapeDtypeStruct((ncb, rows, lanes), F32)
    hre, him = pl.pallas_call(
        functools.partial(_s5_scan_kernel, tcs=tcs),
        grid=(ncb // tcs,),
        in_specs=[seq_spec, seq_spec, par_spec, par_spec],
        out_specs=[seq_spec, seq_spec],
        out_shape=[seq_shape, seq_shape],
        scratch_shapes=[pltpu.VMEM((rows, lanes), F32), pltpu.VMEM((rows, lanes), F32)],
        compiler_params=_cparams("arbitrary"),
        name="s5_scan",
    )(xre.reshape(ncb, rows, lanes), xim.reshape(ncb, rows, lanes), a_tile(a_re_lc), a_tile(a_im_lc))
    hre = hre.reshape(ncb, bsz * g * p)
    him = him.reshape(ncb, bsz * g * p)

    y2 = pl.pallas_call(
        _s5_readout_kernel,
        grid=(g // 2, bsz, nct),
        in_specs=[u_spec,
                  pl.BlockSpec((2, cw, cw), lambda j, b, i: (j, 0, 0)),
                  pl.BlockSpec((2, slab, cw), lambda j, b, i: (j, 0, 0)),
                  pl.BlockSpec((2, slab, cw), lambda j, b, i: (j, 0, 0)),
                  x_spec, x_spec],
        out_specs=u_spec,
        out_shape=jax.ShapeDtypeStruct((g, bsz * ncb, cw), F32),
        compiler_params=_cparams("parallel", "parallel", "parallel"),
        name="s5_readout",
    )(u2, toe, wy_re_p, wy_im_p, hre, him)
    return y2.reshape(g, bsz, ncb, lc, B_GROUP).transpose(1, 2, 3, 0, 4).reshape(bsz, l, B_WIDTH)


def _s5_glu_kernel(yc_ref, u_ref, d_ref, w_ref, o_ref):
    y = jax.nn.gelu(yc_ref[...] + d_ref[...] * u_ref[...])
    z = _dot(y.astype(BF16), w_ref[...])
    o_ref[...] = z[:, :B_WIDTH] * jax.nn.sigmoid(z[:, B_WIDTH:])


def _s5_glu(ycore2d, proj2d, d_skip, w_glu_bf, tm=512):
    t = ycore2d.shape[0]
    ucol = proj2d.shape[1] // B_WIDTH - 1
    return pl.pallas_call(
        _s5_glu_kernel,
        grid=(t // tm,),
        in_specs=[pl.BlockSpec((tm, B_WIDTH), lambda i: (i, 0)),
                  pl.BlockSpec((tm, B_WIDTH), lambda i: (i, ucol)),
                  pl.BlockSpec((1, B_WIDTH), lambda i: (0, 0)),
                  pl.BlockSpec((B_WIDTH, 2 * B_WIDTH), lambda i: (0, 0))],
        out_specs=pl.BlockSpec((tm, B_WIDTH), lambda i: (i, 0)),
        out_shape=jax.ShapeDtypeStruct((t, B_WIDTH), F32),
        compiler_params=_cparams("parallel"),
        name="s5_glu",
    )(ycore2d, proj2d, d_skip.reshape(1, B_WIDTH), w_glu_bf)


def _swa_kernel(sink_ref, q_ref, kp_ref, kc_ref, vp_ref, vc_ref, o_ref):
    first = pl.program_id(1) == 0
    w = WINDOW
    rows = GQA * w
    qpos = lax.broadcasted_iota(jnp.int32, (rows, 2 * w), 0) % w + w
    kpos = lax.broadcasted_iota(jnp.int32, (rows, 2 * w), 1)
    valid = (kpos <= qpos) & (qpos - kpos < w) & jnp.logical_not(first & (kpos < w))
    rowid = lax.broadcasted_iota(jnp.int32, (rows, 1), 0) // w
    scale = HEAD_DIM ** -0.5
    outs = []
    for kv in range(C_KV):
        ls = slice(kv * HEAD_DIM, (kv + 1) * HEAD_DIM)
        k2 = jnp.concatenate([kp_ref[:, ls], kc_ref[:, ls]], axis=0).astype(BF16)
        v2 = jnp.concatenate([vp_ref[:, ls], vc_ref[:, ls]], axis=0).astype(BF16)
        q4 = jnp.concatenate(
            [q_ref[:, (kv * GQA + g) * HEAD_DIM:(kv * GQA + g + 1) * HEAD_DIM] for g in range(GQA)], axis=0)
        s = _nt_dot((q4 * scale).astype(BF16), k2)
        s = jnp.where(valid, s, NEG)
        sink = jnp.zeros((rows, 1), F32)
        for g in range(GQA):
            sink = jnp.where(rowid == g, sink_ref[kv * GQA + g], sink)
        m = jnp.maximum(jnp.max(s, axis=-1, keepdims=True), sink)
        p = jnp.exp(s - m)
        denom = jnp.sum(p, axis=-1, keepdims=True) + jnp.exp(sink - m)
        o = _dot(p.astype(BF16), v2) / denom
        outs.extend(o[g * w:(g + 1) * w, :] for g in range(GQA))
    o_ref[...] = jnp.concatenate(outs, axis=1)


def _swa(proj, sinks):
    bsz, l, _ = proj.shape
    w = WINDOW
    qw = C_HEADS * HEAD_DIM
    kvw = C_KV * HEAD_DIM
    kcol = qw // kvw
    prev = lambda b, n: (b, jnp.maximum(n - 1, 0), kcol)
    prev_v = lambda b, n: (b, jnp.maximum(n - 1, 0), kcol + 1)
    return pl.pallas_call(
        _swa_kernel,
        grid=(bsz, l // w),
        in_specs=[pl.BlockSpec(memory_space=pltpu.SMEM),
                  pl.BlockSpec((None, w, qw), lambda b, n: (b, n, 0)),
                  pl.BlockSpec((None, w, kvw), prev),
                  pl.BlockSpec((None, w, kvw), lambda b, n: (b, n, kcol)),
                  pl.BlockSpec((None, w, kvw), prev_v),
                  pl.BlockSpec((None, w, kvw), lambda b, n: (b, n, kcol + 1))],
        out_specs=pl.BlockSpec((None, w, qw), lambda b, n: (b, n, 0)),
        out_shape=jax.ShapeDtypeStruct((bsz, l, qw), F32),
        compiler_params=_cparams("parallel", "arbitrary"),
        name="swa",
    )(sinks.astype(F32), proj, proj, proj, proj, proj)


def _kmean_kernel(k_ref, o_ref):
    nb = o_ref.shape[0]
    o_ref[...] = jnp.mean(k_ref[...].reshape(nb, MOBA_BLOCK, k_ref.shape[-1]), axis=1)


def _moba_kernel(q_ref, k_ref, v_ref, km_ref, o_ref, m_ref, l_ref, acc_ref):
    kvh = pl.program_id(1)
    i = pl.program_id(2)
    blk = MOBA_BLOCK
    rows = GQA * blk
    nb = km_ref.shape[0]
    scale = HEAD_DIM ** -0.5
    q4 = jnp.concatenate([q_ref[:, g * HEAD_DIM:(g + 1) * HEAD_DIM] for g in range(GQA)], axis=0) * scale
    q_hi, q_lo = _split_bf16(q4)

    km = jnp.where(kvh == 0, km_ref[:, :HEAD_DIM], km_ref[:, HEAD_DIM:])
    km_hi, km_lo = _split_bf16(km)
    gate = _nt_dot(q_hi, km_hi) + _nt_dot(q_lo, km_hi) + _nt_dot(q_hi, km_lo)
    col = lax.broadcasted_iota(jnp.int32, (rows, nb), 1).astype(F32)
    i_f = i.astype(F32)
    past = col < i_f
    gate = jnp.where(past, gate, NEG)
    sel = jnp.zeros((rows, nb), F32)
    for _ in range(min(MOBA_TOPK, nb)):
        top = jnp.max(gate, axis=-1, keepdims=True)
        first = jnp.min(jnp.where(gate == top, col, float(nb)), axis=-1, keepdims=True)
        hit = col == first
        sel = jnp.where(hit & past, 1.0, sel)
        gate = jnp.where(hit, -jnp.inf, gate)

    def kv_rows(ref, j):
        return ref[pl.ds(pl.multiple_of(j * blk, blk), blk), :]

    k_own = kv_rows(k_ref, i)
    v_own = kv_rows(v_ref, i)
    s = _nt_dot(q_hi, k_own)
    qpos = lax.broadcasted_iota(jnp.int32, (rows, blk), 0) % blk
    kpos = lax.broadcasted_iota(jnp.int32, (rows, blk), 1)
    s = jnp.where(kpos <= qpos, s, NEG)
    m0 = jnp.max(s, axis=-1, keepdims=True)
    p = jnp.exp(s - m0)
    m_ref[...] = m0
    l_ref[...] = jnp.sum(p, axis=-1, keepdims=True)
    acc_ref[...] = _dot(p.astype(BF16), v_own)

    def body(j, carry):
        k_j = kv_rows(k_ref, j)
        v_j = kv_rows(v_ref, j)
        picked = jnp.max(jnp.where(col == j.astype(F32), sel, 0.0), axis=-1, keepdims=True) > 0.0
        s_j = jnp.where(picked, _nt_dot(q_hi, k_j), NEG)
        m_old = m_ref[...]
        m_new = jnp.maximum(m_old, jnp.max(s_j, axis=-1, keepdims=True))
        p_j = jnp.exp(s_j - m_new)
        corr = jnp.exp(m_old - m_new)
        m_ref[...] = m_new
        l_ref[...] = l_ref[...] * corr + jnp.sum(p_j, axis=-1, keepdims=True)
        acc_ref[...] = acc_ref[...] * corr + _dot(p_j.astype(BF16), v_j)
        return carry

    lax.fori_loop(0, i, body, 0)
    out = acc_ref[...] / l_ref[...]
    o_ref[...] = jnp.concatenate([out[g * blk:(g + 1) * blk, :] for g in range(GQA)], axis=1)


def _moba(proj):
    bsz, l, _ = proj.shape
    blk = MOBA_BLOCK
    nb = l // blk
    kvw = D_KV * HEAD_DIM
    qgw = GQA * HEAD_DIM
    q0 = C_HEADS * HEAD_DIM + 2 * C_KV * HEAD_DIM
    k0 = q0 + D_HEADS * HEAD_DIM
    qcol0 = q0 // qgw
    kmean = pl.pallas_call(
        _kmean_kernel,
        grid=(bsz,),
        in_specs=[pl.BlockSpec((None, l, kvw), lambda b: (b, 0, k0 // kvw))],
        out_specs=pl.BlockSpec((None, nb, kvw), lambda b: (b, 0, 0)),
        out_shape=jax.ShapeDtypeStruct((bsz, nb, kvw), F32),
        compiler_params=_cparams("parallel"),
        name="moba_kmean",
    )(proj)
    to_heads = lambda c0: proj[:, :, c0:c0 + kvw].reshape(bsz, l, D_KV, HEAD_DIM).transpose(0, 2, 1, 3).astype(BF16)
    k_bf, v_bf = to_heads(k0), to_heads(k0 + kvw)
    rows = GQA * blk
    return pl.pallas_call(
        _moba_kernel,
        grid=(bsz, D_KV, nb),
        in_specs=[pl.BlockSpec((None, blk, qgw), lambda b, h, i: (b, i, qcol0 + h)),
                  pl.BlockSpec((None, None, l, HEAD_DIM), lambda b, h, i: (b, h, 0, 0)),
                  pl.BlockSpec((None, None, l, HEAD_DIM), lambda b, h, i: (b, h, 0, 0)),
                  pl.BlockSpec((None, nb, kvw), lambda b, h, i: (b, 0, 0))],
        out_specs=pl.BlockSpec((None, blk, qgw), lambda b, h, i: (b, i, h)),
        out_shape=jax.ShapeDtypeStruct((bsz, l, D_HEADS * HEAD_DIM), F32),
        scratch_shapes=[pltpu.VMEM((rows, 1), F32), pltpu.VMEM((rows, 1), F32),
                        pltpu.VMEM((rows, HEAD_DIM), F32)],
        compiler_params=_cparams("parallel", "arbitrary", "arbitrary"),
        name="moba",
    )(proj, k_bf, v_bf, kmean)


ROUTER_LANES = 128


def _moe_gates(x, wr_ref, br_ref):
    x_hi, x_lo = _split_bf16(x)
    w_hi, w_lo = _split_bf16(wr_ref[...])
    logits = _dot(x_hi, w_hi) + _dot(x_lo, w_hi) + _dot(x_hi, w_lo) + br_ref[...]
    lane = lax.broadcasted_iota(jnp.int32, logits.shape, 1).astype(F32)
    big = float(ROUTER_LANES)
    gl = jnp.where(lane < N_GROUPS, logits, -jnp.inf)
    gmax = jnp.max(gl, axis=-1, keepdims=True)
    g_top = 1.0 / jnp.sum(jnp.exp(gl - gmax), axis=-1, keepdims=True)
    g_idx = jnp.min(jnp.where(gl == gmax, lane, big), axis=-1, keepdims=True)
    lo = N_GROUPS + EXP_PER_GROUP * g_idx
    el = jnp.where((lane >= lo) & (lane < lo + EXP_PER_GROUP), logits, -jnp.inf)
    m1 = jnp.max(el, axis=-1, keepdims=True)
    i1 = jnp.min(jnp.where(el == m1, lane, big), axis=-1, keepdims=True)
    el2 = jnp.where(lane == i1, -jnp.inf, el)
    m2 = jnp.max(el2, axis=-1, keepdims=True)
    i2 = jnp.min(jnp.where(el2 == m2, lane, big), axis=-1, keepdims=True)
    e21 = jnp.exp(m2 - m1)
    w1 = g_top / (1.0 + e21)
    w2 = w1 * e21
    return jnp.where(lane == i1, w1, 0.0) + jnp.where(lane == i2, w2, 0.0)


def _moe_kernel(x_ref, wr_ref, br_ref, wgu_ref, wd_ref, g_ref, b_ref, o_ref, gates_ref, xb_ref, acc_ref, *, alpha):
    e = pl.program_id(1)

    @pl.when(e == 0)
    def _():
        x = x_ref[...]
        gates_ref[...] = _moe_gates(x, wr_ref, br_ref)
        xb_ref[...] = x.astype(BF16)
        acc_ref[...] = jnp.zeros_like(acc_ref)

    gates = gates_ref[...]
    lane = lax.broadcasted_iota(jnp.int32, gates.shape, 1)
    gate_e = jnp.sum(jnp.where(lane == N_GROUPS + e, gates, 0.0), axis=-1, keepdims=True)
    hu = _dot(xb_ref[...], wgu_ref[...])
    h = _silu(hu[:, :EXP_HIDDEN]) * hu[:, EXP_HIDDEN:] * gate_e
    acc_ref[...] += _dot(h.astype(BF16), wd_ref[...])

    @pl.when(e == N_EXPERTS - 1)
    def _():
        o_ref[...] = _layer_norm_rows(alpha * x_ref[...] + acc_ref[...], g_ref[...], b_ref[...])


def _moe_ln(x2d, w_group, b_group, w_expert, b_expert, wgu_bf, wd_bf, g, b, alpha, tm=1024):
    t, d = x2d.shape
    tm = min(tm, t)
    pad = ROUTER_LANES - N_GROUPS - N_EXPERTS
    w_router = jnp.concatenate([w_group, w_expert, jnp.zeros((d, pad), F32)], axis=1).astype(F32)
    b_router = jnp.concatenate([b_group, b_expert, jnp.zeros((pad,), F32)]).astype(F32).reshape(1, ROUTER_LANES)
    return pl.pallas_call(
        functools.partial(_moe_kernel, alpha=alpha),
        grid=(t // tm, N_EXPERTS),
        in_specs=[pl.BlockSpec((tm, d), lambda i, e: (i, 0)),
                  pl.BlockSpec((d, ROUTER_LANES), lambda i, e: (0, 0)),
                  pl.BlockSpec((1, ROUTER_LANES), lambda i, e: (0, 0)),
                  pl.BlockSpec((None, d, 2 * EXP_HIDDEN), lambda i, e: (e, 0, 0)),
                  pl.BlockSpec((None, EXP_HIDDEN, d), lambda i, e: (e, 0, 0)),
                  pl.BlockSpec((1, d), lambda i, e: (0, 0)),
                  pl.BlockSpec((1, d), lambda i, e: (0, 0))],
        out_specs=pl.BlockSpec((tm, d), lambda i, e: (i, 0)),
        out_shape=jax.ShapeDtypeStruct((t, d), F32),
        scratch_shapes=[pltpu.VMEM((tm, ROUTER_LANES), F32), pltpu.VMEM((tm, d), BF16),
                        pltpu.VMEM((tm, d), F32)],
        compiler_params=_cparams("parallel", "arbitrary"),
        name="moe_ln",
    )(x2d, w_router, b_router, wgu_bf, wd_bf, g.reshape(1, d), b.reshape(1, d))


def kernel(x, hgrn_lb_logits, ev_w_in, ev_a_norm, ev_s5_a_re, ev_s5_a_im, ev_s5_log_dt, ev_s5_b_re, ev_s5_b_im, ev_s5_c_re, ev_s5_c_im, ev_s5_d, ev_s5_w_glu, ev_w_out, od_w_in, od_sinks, od_w_out, ln1_g, ln1_b, moe_w_group, moe_b_group, moe_w_expert, moe_b_expert, moe_w_gate_up, moe_w_down, ln2_g, ln2_b):
    bsz, l, d = x.shape
    depth = ln1_g.shape[0]
    alpha = (2.0 * depth) ** 0.25
    t = bsz * l
    lower_bounds = jnp.cumsum(jax.nn.softmax(hgrn_lb_logits.astype(F32), axis=0), axis=0)
    x2d = x.reshape(t, d)
    for layer in range(depth):
        j = layer // 2
        if layer % 2 == 0:
            proj = _proj(x2d, ev_w_in[j].astype(BF16))
            proj3 = proj.reshape(bsz, l, proj.shape[1])
            ya = _hgrn2(proj3, lower_bounds[layer], ev_a_norm[j])
            ycore = _s5_core(proj3[:, :, 4 * A_W:], ev_s5_a_re[j], ev_s5_a_im[j], ev_s5_log_dt[j],
                             ev_s5_b_re[j], ev_s5_b_im[j], ev_s5_c_re[j], ev_s5_c_im[j])
            yb = _s5_glu(ycore.reshape(t, B_WIDTH), proj, ev_s5_d[j], ev_s5_w_glu[j].astype(BF16))
            x2d = _mix_ln(ya.reshape(t, A_W), yb, x2d, ev_w_out[j].astype(BF16), ln1_g[layer], ln1_b[layer], alpha)
        else:
            proj = _proj(x2d, od_w_in[j].astype(BF16))
            proj3 = proj.reshape(bsz, l, proj.shape[1])
            yc = _swa(proj3, od_sinks[j])
            yd = _moba(proj3)
            x2d = _mix_ln(yc.reshape(t, -1), yd.reshape(t, -1), x2d, od_w_out[j].astype(BF16),
                          ln1_g[layer], ln1_b[layer], alpha)
        x2d = _moe_ln(x2d, moe_w_group[layer], moe_b_group[layer], moe_w_expert[layer], moe_b_expert[layer],
                      moe_w_gate_up[layer].astype(BF16), moe_w_down[layer].astype(BF16),
                      ln2_g[layer], ln2_b[layer], alpha)
    return x2d.reshape(bsz, l, d)
```

```python
import functools
import math

import jax
import jax.numpy as jnp
from jax import lax
from jax.experimental import pallas as pl
from jax.experimental.pallas import tpu as pltpu

F32 = jnp.float32
BF16 = jnp.bfloat16
NEG = -1e30
LN_EPS = 1e-5
RMS_EPS = 1e-6

A_HEADS = 4
A_DK = 128
A_DV = 128
A_CHUNK = 64
A_W = A_HEADS * A_DK

B_WIDTH = 512
B_GROUP = 16
B_GROUPS = B_WIDTH // B_GROUP
B_STATE = 64
S5_LC = 16
S5_CW = S5_LC * B_GROUP

HEAD_DIM = 64
C_HEADS = 8
C_KV = 2
WINDOW = 128
D_HEADS = 8
D_KV = 2
MOBA_BLOCK = 256
MOBA_TOPK = 3
LOG2E = math.log2(math.e)
BF16_SUBLANES = 16
GQA = C_HEADS // C_KV

N_GROUPS = 4
EXP_PER_GROUP = 4
N_EXPERTS = 16
EXP_HIDDEN = 256

VMEM_LIMIT_BYTES = 48 * 1024 * 1024


def _cparams(*sem):
    return pltpu.CompilerParams(dimension_semantics=sem, vmem_limit_bytes=VMEM_LIMIT_BYTES)


def _nt_dot(a, b):
    return lax.dot_general(a, b, (((1,), (1,)), ((), ())), preferred_element_type=F32)


def _dot(a, b):
    return jnp.dot(a, b, preferred_element_type=F32)


def _split_bf16(v):
    hi = v.astype(BF16)
    lo = (v - hi.astype(F32)).astype(BF16)
    return hi, lo


def _silu(v):
    return v * jax.nn.sigmoid(v)


def _layer_norm_rows(r, g, b):
    mu = jnp.mean(r, axis=-1, keepdims=True)
    d = r - mu
    var = jnp.mean(d * d, axis=-1, keepdims=True)
    return d * lax.rsqrt(var + LN_EPS) * g + b


def _proj_kernel(x_ref, w_ref, o_ref):
    o_ref[...] = _dot(x_ref[...].astype(BF16), w_ref[...])


def _proj(x2d, w_bf, tm=512):
    t, d = x2d.shape
    n = w_bf.shape[1]
    return pl.pallas_call(
        _proj_kernel,
        grid=(t // tm,),
        in_specs=[pl.BlockSpec((tm, d), lambda i: (i, 0)),
                  pl.BlockSpec((d, n), lambda i: (0, 0))],
        out_specs=pl.BlockSpec((tm, n), lambda i: (i, 0)),
        out_shape=jax.ShapeDtypeStruct((t, n), F32),
        compiler_params=_cparams("parallel"),
        name="in_proj",
    )(x2d, w_bf)


def _mix_ln_kernel(ya_ref, yb_ref, x_ref, w_ref, g_ref, b_ref, o_ref, *, alpha):
    half = ya_ref.shape[1]
    mix = _dot(ya_ref[...].astype(BF16), w_ref[0:half, :]) + _dot(yb_ref[...].astype(BF16), w_ref[half:, :])
    o_ref[...] = _layer_norm_rows(alpha * x_ref[...] + mix, g_ref[...], b_ref[...])


def _mix_ln(ya, yb, x2d, w_out_bf, g, b, alpha, tm=512):
    t, d = x2d.shape
    half = ya.shape[1]
    return pl.pallas_call(
        functools.partial(_mix_ln_kernel, alpha=alpha),
        grid=(t // tm,),
        in_specs=[pl.BlockSpec((tm, half), lambda i: (i, 0)),
                  pl.BlockSpec((tm, half), lambda i: (i, 0)),
                  pl.BlockSpec((tm, d), lambda i: (i, 0)),
                  pl.BlockSpec((2 * half, d), lambda i: (0, 0)),
                  pl.BlockSpec((1, d), lambda i: (0, 0)),
                  pl.BlockSpec((1, d), lambda i: (0, 0))],
        out_specs=pl.BlockSpec((tm, d), lambda i: (i, 0)),
        out_shape=jax.ShapeDtypeStruct((t, d), F32),
        compiler_params=_cparams("parallel"),
        name="mix_ln",
    )(ya, yb, x2d, w_out_bf, g.reshape(1, d), b.reshape(1, d))


def _hgrn2_kernel(proj_ref, lb_ref, ng_ref, tri_ref, o_ref, st_ref, *, tb):
    @pl.when(pl.program_id(1) == 0)
    def _():
        st_ref[...] = jnp.zeros_like(st_ref)

    ql = proj_ref[:, 0:A_W]
    fl = proj_ref[:, A_W:2 * A_W]
    iv = proj_ref[:, 2 * A_W:3 * A_W]
    gt = proj_ref[:, 3 * A_W:4 * A_W]
    lb = lb_ref[...]
    f = lb + (1.0 - lb) * jax.nn.sigmoid(fl)
    k = 1.0 - f
    lf = jnp.log(f)
    lf_hi, lf_lo = _split_bf16(lf)
    tri = tri_ref[...]
    b = _dot(tri, lf_hi) + _dot(tri, lf_lo)
    qd = _silu(ql) * jnp.exp(b)
    ki = k * jnp.exp(-b)
    og = ng_ref[...] * _silu(gt)

    row = lax.broadcasted_iota(jnp.int32, (A_CHUNK, A_CHUNK), 0)
    col = lax.broadcasted_iota(jnp.int32, (A_CHUNK, A_CHUNK), 1)
    causal = col <= row
    v_t = [iv[:, h * A_DV:(h + 1) * A_DV].T for h in range(A_HEADS)]

    for c in range(tb // A_CHUNK):
        r0 = c * A_CHUNK
        b_c = b[r0:r0 + A_CHUNK, :]
        b_last = b[r0 + A_CHUNK - 1:r0 + A_CHUNK, :]
        kt = k[r0:r0 + A_CHUNK, :] * jnp.exp(b_last - b_c)
        dec = jnp.exp(b_last)
        for h in range(A_HEADS):
            ls = slice(h * A_DK, (h + 1) * A_DK)
            qd_h = qd[r0:r0 + A_CHUNK, ls].astype(BF16)
            ki_h = ki[r0:r0 + A_CHUNK, ls].astype(BF16)
            v_h = iv[r0:r0 + A_CHUNK, ls]
            att = jnp.where(causal, _nt_dot(qd_h, ki_h), 0.0)
            s_t = st_ref[h]
            o = _dot(att.astype(BF16), v_h.astype(BF16)) + _nt_dot(qd_h, s_t.astype(BF16))
            st_ref[h] = s_t * dec[:, ls] + _dot(v_t[h][:, r0:r0 + A_CHUNK].astype(BF16), kt[:, ls].astype(BF16))
            o = o * lax.rsqrt(jnp.mean(o * o, axis=-1, keepdims=True) + RMS_EPS)
            o_ref[r0:r0 + A_CHUNK, ls] = o * og[r0:r0 + A_CHUNK, ls]


def _hgrn2(proj, lb, norm_g, tb=256):
    bsz, l, _ = proj.shape
    tb = min(tb, l)
    idx = jnp.arange(tb)
    tri = ((idx[:, None] >= idx[None, :]) & (idx[:, None] // A_CHUNK == idx[None, :] // A_CHUNK)).astype(BF16)
    return pl.pallas_call(
        functools.partial(_hgrn2_kernel, tb=tb),
        grid=(bsz, l // tb),
        in_specs=[pl.BlockSpec((None, tb, 4 * A_W), lambda b, t: (b, t, 0)),
                  pl.BlockSpec((1, A_W), lambda b, t: (0, 0)),
                  pl.BlockSpec((1, A_W), lambda b, t: (0, 0)),
                  pl.BlockSpec((tb, tb), lambda b, t: (0, 0))],
        out_specs=pl.BlockSpec((None, tb, A_W), lambda b, t: (b, t, 0)),
        out_shape=jax.ShapeDtypeStruct((bsz, l, A_W), F32),
        scratch_shapes=[pltpu.VMEM((A_HEADS, A_DV, A_DK), F32)],
        compiler_params=_cparams("parallel", "arbitrary"),
        name="hgrn2",
    )(proj, lb.reshape(1, A_W), norm_g.reshape(1, A_W), tri)


def _s5_weights(a_re, a_im, log_dt, b_re, b_im, c_re, c_im):
    g, p = a_re.shape
    lc = S5_LC
    dt = jnp.exp(log_dt.astype(F32))[:, None]
    ar, ai = a_re.astype(F32), a_im.astype(F32)
    mag = jnp.exp(dt * ar)
    abar_re, abar_im = mag * jnp.cos(dt * ai), mag * jnp.sin(dt * ai)
    den = ar * ar + ai * ai
    xr, xi = abar_re - 1.0, abar_im
    fr = (xr * ar + xi * ai) / den
    fi = (xi * ar - xr * ai) / den
    br, bi = b_re.astype(F32), b_im.astype(F32)
    bb_re = fr[..., None] * br - fi[..., None] * bi
    bb_im = fr[..., None] * bi + fi[..., None] * br
    cr, ci = c_re.astype(F32), c_im.astype(F32)
    tau = jnp.arange(lc + 1, dtype=F32)[:, None, None]
    pmag = jnp.exp(tau * (dt * ar)[None])
    pw_re = pmag * jnp.cos(tau * (dt * ai)[None])
    pw_im = pmag * jnp.sin(tau * (dt * ai)[None])
    hp = lax.Precision.HIGHEST
    ab_re = pw_re[..., None] * bb_re[None] - pw_im[..., None] * bb_im[None]
    ab_im = pw_re[..., None] * bb_im[None] + pw_im[..., None] * bb_re[None]
    kk = (jnp.einsum('gnp,tgpm->tgnm', cr, ab_re, precision=hp)
          - jnp.einsum('gnp,tgpm->tgnm', ci, ab_im, precision=hp))
    s_idx = jnp.arange(lc)[:, None]
    t_idx = jnp.arange(lc)[None, :]
    lag = t_idx - s_idx
    toe = jnp.where((lag >= 0)[:, :, None, None, None], kk[jnp.clip(lag, 0, lc)], 0.0)
    toe = toe.transpose(2, 0, 4, 1, 3).reshape(g, lc * B_GROUP, lc * B_GROUP)
    rev = lc - 1 - jnp.arange(lc)
    wx_re = ab_re[rev].transpose(1, 0, 3, 2).reshape(g, lc * B_GROUP, p)
    wx_im = ab_im[rev].transpose(1, 0, 3, 2).reshape(g, lc * B_GROUP, p)
    pr, pi_ = pw_re[1:], pw_im[1:]
    wy_re = cr[None] * pr[:, :, None, :] - ci[None] * pi_[:, :, None, :]
    wy_im = -(cr[None] * pi_[:, :, None, :] + ci[None] * pr[:, :, None, :])
    wy_re = wy_re.transpose(1, 3, 0, 2).reshape(g, p, lc * B_GROUP)
    wy_im = wy_im.transpose(1, 3, 0, 2).reshape(g, p, lc * B_GROUP)
    return toe, wx_re, wx_im, wy_re, wy_im, pw_re[lc], pw_im[lc]


def _pair_pack_rows(w):
    g, r, c = w.shape
    w = w.reshape(g // 2, 2, r, c)
    z = jnp.zeros((g // 2, r, c), w.dtype)
    top = jnp.concatenate([w[:, 0], z], axis=2)
    bot = jnp.concatenate([z, w[:, 1]], axis=2)
    return jnp.concatenate([top, bot], axis=1)


def _s5_inject_kernel(u_ref, wre_ref, wim_ref, xre_ref, xim_ref):
    lhs = jnp.concatenate([u_ref[0], u_ref[1]], axis=1).astype(BF16)
    xre_ref[...] = _dot(lhs, wre_ref[...])
    xim_ref[...] = _dot(lhs, wim_ref[...])


def _s5_scan_kernel(xre_ref, xim_ref, are_ref, aim_ref, hre_ref, him_ref, sre_ref, sim_ref, *, tcs):
    @pl.when(pl.program_id(0) == 0)
    def _():
        sre_ref[...] = jnp.zeros_like(sre_ref)
        sim_ref[...] = jnp.zeros_like(sim_ref)

    ar = are_ref[...]
    ai = aim_ref[...]

    def body(r, carry):
        hr, hi = carry
        hre_ref[r] = hr
        him_ref[r] = hi
        return ar * hr - ai * hi + xre_ref[r], ar * hi + ai * hr + xim_ref[r]

    hr, hi = lax.fori_loop(0, tcs, body, (sre_ref[...], sim_ref[...]), unroll=8)
    sre_ref[...] = hr
    sim_ref[...] = hi


def _s5_readout_kernel(u_ref, toe_ref, wyre_ref, wyim_ref, hre_ref, him_ref, y_ref):
    hre = hre_ref[...].astype(BF16)
    him = him_ref[...].astype(BF16)
    for r in range(2):
        y_ref[r] = (_dot(u_ref[r].astype(BF16), toe_ref[r])
                    + _dot(hre, wyre_ref[r]) + _dot(him, wyim_ref[r]))


def _s5_core(u, a_re, a_im, log_dt, b_re, b_im, c_re, c_im):
    bsz, l, _ = u.shape
    g, p, lc, cw = B_GROUPS, B_STATE, S5_LC, S5_CW
    ncb = l // lc
    toe, wx_re, wx_im, wy_re, wy_im, a_re_lc, a_im_lc = _s5_weights(a_re, a_im, log_dt, b_re, b_im, c_re, c_im)
    toe = toe.astype(BF16)
    wx_re_p = _pair_pack_rows(wx_re).astype(BF16)
    wx_im_p = _pair_pack_rows(wx_im).astype(BF16)
    zero = jnp.zeros_like(wy_re)
    even = (jnp.arange(g) % 2 == 0)[:, None, None]
    wy_re_p = jnp.where(even, jnp.concatenate([wy_re, zero], 1), jnp.concatenate([zero, wy_re], 1)).astype(BF16)
    wy_im_p = jnp.where(even, jnp.concatenate([wy_im, zero], 1), jnp.concatenate([zero, wy_im], 1)).astype(BF16)

    u2 = u.reshape(bsz, ncb, lc, g, B_GROUP).transpose(3, 0, 1, 2, 4).reshape(g, bsz * ncb, cw)
    tc = min(1024, ncb)
    nct = ncb // tc
    slab = 2 * p
    x_shape = jax.ShapeDtypeStruct((ncb, bsz * g * p), F32)
    x_spec = pl.BlockSpec((tc, slab), lambda j, b, i: (i, b * (g // 2) + j))
    u_spec = pl.BlockSpec((2, tc, cw), lambda j, b, i: (j, b * nct + i, 0))
    xre, xim = pl.pallas_call(
        _s5_inject_kernel,
        grid=(g // 2, bsz, nct),
        in_specs=[u_spec,
                  pl.BlockSpec((None, 2 * cw, slab), lambda j, b, i: (j, 0, 0)),
                  pl.BlockSpec((None, 2 * cw, slab), lambda j, b, i: (j, 0, 0))],
        out_specs=[x_spec, x_spec],
        out_shape=[x_shape, x_shape],
        compiler_params=_cparams("parallel", "parallel", "parallel"),
        name="s5_inject",
    )(u2, wx_re_p, wx_im_p)

    lanes = 512
    rows = bsz * g * p // lanes
    tcs = min(128, ncb)
    a_tile = lambda a: jnp.tile(a.reshape(g * p // lanes, lanes), (bsz, 1))
    seq_spec = pl.BlockSpec((tcs, rows, lanes), lambda i: (i, 0, 0))
    par_spec = pl.BlockSpec((rows, lanes), lambda i: (0, 0))
    seq_shape = jax.ShapeDtypeStruct((ncb, rows, lanes), F32)
    hre, him = pl.pallas_call(
        functools.partial(_s5_scan_kernel, tcs=tcs),
        grid=(ncb // tcs,),
        in_specs=[seq_spec, seq_spec, par_spec, par_spec],
        out_specs=[seq_spec, seq_spec],
        out_shape=[seq_shape, seq_shape],
        scratch_shapes=[pltpu.VMEM((rows, lanes), F32), pltpu.VMEM((rows, lanes), F32)],
        compiler_params=_cparams("arbitrary"),
        name="s5_scan",
    )(xre.reshape(ncb, rows, lanes), xim.reshape(ncb, rows, lanes), a_tile(a_re_lc), a_tile(a_im_lc))
    hre = hre.reshape(ncb, bsz * g * p)
    him = him.reshape(ncb, bsz * g * p)

    y2 = pl.pallas_call(
        _s5_readout_kernel,
        grid=(g // 2, bsz, nct),
        in_specs=[u_spec,
                  pl.BlockSpec((2, cw, cw), lambda j, b, i: (j, 0, 0)),
                  pl.BlockSpec((2, slab, cw), lambda j, b, i: (j, 0, 0)),
                  pl.BlockSpec((2, slab, cw), lambda j, b, i: (j, 0, 0)),
                  x_spec, x_spec],
        out_specs=u_spec,
        out_shape=jax.ShapeDtypeStruct((g, bsz * ncb, cw), F32),
        compiler_params=_cparams("parallel", "parallel", "parallel"),
        name="s5_readout",
    )(u2, toe, wy_re_p, wy_im_p, hre, him)
    return y2.reshape(g, bsz, ncb, lc, B_GROUP).transpose(1, 2, 3, 0, 4).reshape(bsz, l, B_WIDTH)


def _s5_glu_kernel(yc_ref, u_ref, d_ref, w_ref, o_ref):
    y = jax.nn.gelu(yc_ref[...] + d_ref[...] * u_ref[...])
    z = _dot(y.astype(BF16), w_ref[...])
    o_ref[...] = z[:, :B_WIDTH] * jax.nn.sigmoid(z[:, B_WIDTH:])


def _s5_glu(ycore2d, proj2d, d_skip, w_glu_bf, tm=512):
    t = ycore2d.shape[0]
    ucol = proj2d.shape[1] // B_WIDTH - 1
    return pl.pallas_call(
        _s5_glu_kernel,
        grid=(t // tm,),
        in_specs=[pl.BlockSpec((tm, B_WIDTH), lambda i: (i, 0)),
                  pl.BlockSpec((tm, B_WIDTH), lambda i: (i, ucol)),
                  pl.BlockSpec((1, B_WIDTH), lambda i: (0, 0)),
                  pl.BlockSpec((B_WIDTH, 2 * B_WIDTH), lambda i: (0, 0))],
        out_specs=pl.BlockSpec((tm, B_WIDTH), lambda i: (i, 0)),
        out_shape=jax.ShapeDtypeStruct((t, B_WIDTH), F32),
        compiler_params=_cparams("parallel"),
        name="s5_glu",
    )(ycore2d, proj2d, d_skip.reshape(1, B_WIDTH), w_glu_bf)


def _swa_kernel(sink_ref, q_ref, kp_ref, kc_ref, vp_ref, vc_ref, o_ref):
    first = pl.program_id(1) == 0
    w = WINDOW
    rows = GQA * w
    qpos = lax.broadcasted_iota(jnp.int32, (rows, 2 * w), 0) % w + w
    kpos = lax.broadcasted_iota(jnp.int32, (rows, 2 * w), 1)
    valid = (kpos <= qpos) & (qpos - kpos < w) & jnp.logical_not(first & (kpos < w))
    rowid = lax.broadcasted_iota(jnp.int32, (rows, 1), 0) // w
    scale = HEAD_DIM ** -0.5
    outs = []
    for kv in range(C_KV):
        ls = slice(kv * HEAD_DIM, (kv + 1) * HEAD_DIM)
        k2 = jnp.concatenate([kp_ref[:, ls], kc_ref[:, ls]], axis=0).astype(BF16)
        v2 = jnp.concatenate([vp_ref[:, ls], vc_ref[:, ls]], axis=0).astype(BF16)
        q4 = jnp.concatenate(
            [q_ref[:, (kv * GQA + g) * HEAD_DIM:(kv * GQA + g + 1) * HEAD_DIM] for g in range(GQA)], axis=0)
        s = _nt_dot((q4 * scale).astype(BF16), k2)
        s = jnp.where(valid, s, NEG)
        sink = jnp.zeros((rows, 1), F32)
        for g in range(GQA):
            sink = jnp.where(rowid == g, sink_ref[kv * GQA + g], sink)
        m = jnp.maximum(jnp.max(s, axis=-1, keepdims=True), sink)
        p = jnp.exp(s - m)
        denom = jnp.sum(p, axis=-1, keepdims=True) + jnp.exp(sink - m)
        o = _dot(p.astype(BF16), v2) / denom
        outs.extend(o[g * w:(g + 1) * w, :] for g in range(GQA))
    o_ref[...] = jnp.concatenate(outs, axis=1)


def _swa(proj, sinks):
    bsz, l, _ = proj.shape
    w = WINDOW
    qw = C_HEADS * HEAD_DIM
    kvw = C_KV * HEAD_DIM
    kcol = qw // kvw
    prev = lambda b, n: (b, jnp.maximum(n - 1, 0), kcol)
    prev_v = lambda b, n: (b, jnp.maximum(n - 1, 0), kcol + 1)
    return pl.pallas_call(
        _swa_kernel,
        grid=(bsz, l // w),
        in_specs=[pl.BlockSpec(memory_space=pltpu.SMEM),
                  pl.BlockSpec((None, w, qw), lambda b, n: (b, n, 0)),
                  pl.BlockSpec((None, w, kvw), prev),
                  pl.BlockSpec((None, w, kvw), lambda b, n: (b, n, kcol)),
                  pl.BlockSpec((None, w, kvw), prev_v),
                  pl.BlockSpec((None, w, kvw), lambda b, n: (b, n, kcol + 1))],
        out_specs=pl.BlockSpec((None, w, qw), lambda b, n: (b, n, 0)),
        out_shape=jax.ShapeDtypeStruct((bsz, l, qw), F32),
        compiler_params=_cparams("parallel", "arbitrary"),
        name="swa",
    )(sinks.astype(F32), proj, proj, proj, proj, proj)


def _kmean_kernel(k_ref, o_ref):
    nb = o_ref.shape[0]
    o_ref[...] = jnp.mean(k_ref[...].reshape(nb, MOBA_BLOCK, k_ref.shape[-1]), axis=1)


def _moba_kernel(q_ref, k_ref, vt_ref, km_ref, o_ref, qaug_ref, acc_ref, sa_ref, sb_ref):
    kvh = pl.program_id(1)
    i = pl.program_id(2)
    blk = MOBA_BLOCK
    cols = GQA * blk
    nb = km_ref.shape[0]
    scale = HEAD_DIM ** -0.5
    qt = q_ref[...].T * scale
    q_t = jnp.concatenate([qt[g * HEAD_DIM:(g + 1) * HEAD_DIM, :] for g in range(GQA)], axis=1)
    q_hi, q_lo = _split_bf16(q_t)

    km = jnp.where(kvh == 0, km_ref[:, :HEAD_DIM], km_ref[:, HEAD_DIM:])
    km_hi, km_lo = _split_bf16(km)
    gate = _dot(km_hi, q_hi) + _dot(km_hi, q_lo) + _dot(km_lo, q_hi)
    blkid = lax.broadcasted_iota(jnp.int32, (nb, cols), 0).astype(F32)
    past = blkid < i.astype(F32)
    gate = jnp.where(past, gate, NEG)
    sel = jnp.zeros((nb, cols), F32)
    for _ in range(min(MOBA_TOPK, nb)):
        top = jnp.max(gate, axis=0, keepdims=True)
        first = jnp.min(jnp.where(gate == top, blkid, float(nb)), axis=0, keepdims=True)
        hit = blkid == first
        sel = jnp.where(hit & past, 1.0, sel)
        gate = jnp.where(hit, -jnp.inf, gate)
    q2 = (q_t * LOG2E).astype(BF16)
    qaug_ref[0:HEAD_DIM, :] = q2
    pad = qaug_ref.shape[0] - HEAD_DIM - nb
    mask = jnp.where(sel > 0.0, 0.0, NEG)
    if pad:
        mask = jnp.concatenate([mask, jnp.zeros((pad, cols), F32)], axis=0)
    qaug_ref[HEAD_DIM:, :] = mask.astype(BF16)

    def k_rows(j):
        return k_ref[pl.ds(pl.multiple_of(j * blk, blk), blk), :]

    def scores(j):
        return _dot(k_rows(j), qaug_ref[...])

    s = _dot(k_rows(i)[:, :HEAD_DIM], q2)
    kpos = lax.broadcasted_iota(jnp.int32, (blk, cols), 0)
    qpos = lax.broadcasted_iota(jnp.int32, (blk, cols), 1) % blk
    s = jnp.where(kpos <= qpos, s, NEG)
    m0 = jnp.max(s, axis=0, keepdims=True)
    p = jnp.exp2(s - m0)
    acc_ref[...] = _dot(vt_ref[i], p.astype(BF16))

    def stage(s_ref, j):
        s_j = scores(j)
        s_ref[...] = s_j
        return jnp.max(s_j, axis=0, keepdims=True)

    def absorb(s_ref, s_max, j, m_old):
        m_new = jnp.maximum(m_old, s_max)
        p_j = jnp.exp2(s_ref[...] - m_new)
        corr = jnp.exp2(m_old - m_new)
        acc_ref[...] = acc_ref[...] * corr + _dot(vt_ref[j], p_j.astype(BF16))
        return m_new

    def body(t, carry):
        m, max_a = carry
        j0 = 2 * t
        max_b = stage(sb_ref, j0 + 1)
        m = absorb(sa_ref, max_a, j0, m)
        max_a = stage(sa_ref, jnp.minimum(j0 + 2, nb - 1))
        m = absorb(sb_ref, max_b, j0 + 1, m)
        return m, max_a

    lax.fori_loop(0, (i + 1) // 2, body, (m0, stage(sa_ref, 0)))
    out_t = acc_ref[0:HEAD_DIM, :] / acc_ref[HEAD_DIM:HEAD_DIM + 1, :]
    stacked = jnp.concatenate([out_t[:, g * blk:(g + 1) * blk] for g in range(GQA)], axis=0)
    o_ref[...] = stacked.T


def _moba(proj):
    bsz, l, _ = proj.shape
    blk = MOBA_BLOCK
    nb = l // blk
    kvw = D_KV * HEAD_DIM
    qgw = GQA * HEAD_DIM
    q0 = C_HEADS * HEAD_DIM + 2 * C_KV * HEAD_DIM
    k0 = q0 + D_HEADS * HEAD_DIM
    qcol0 = q0 // qgw
    kmean = pl.pallas_call(
        _kmean_kernel,
        grid=(bsz,),
        in_specs=[pl.BlockSpec((None, l, kvw), lambda b: (b, 0, k0 // kvw))],
        out_specs=pl.BlockSpec((None, nb, kvw), lambda b: (b, 0, 0)),
        out_shape=jax.ShapeDtypeStruct((bsz, nb, kvw), F32),
        compiler_params=_cparams("parallel"),
        name="moba_kmean",
    )(proj)
    nbp = -(-nb // BF16_SUBLANES) * BF16_SUBLANES
    k_bf = proj[:, :, k0:k0 + kvw].reshape(bsz, l, D_KV, HEAD_DIM).transpose(0, 2, 1, 3).astype(BF16)
    onehot = (jnp.arange(l)[:, None] // blk == jnp.arange(nbp)[None, :]).astype(BF16)
    k_aug = jnp.concatenate([k_bf, jnp.broadcast_to(onehot, (bsz, D_KV, l, nbp))], axis=-1)
    vt_bf = (proj[:, :, k0 + kvw:k0 + 2 * kvw].reshape(bsz, nb, blk, D_KV, HEAD_DIM)
             .transpose(0, 3, 1, 4, 2).astype(BF16))
    ones_rows = jnp.zeros((bsz, D_KV, nb, BF16_SUBLANES, blk), BF16).at[:, :, :, 0, :].set(1.0)
    vt_bf = jnp.concatenate([vt_bf, ones_rows], axis=3)
    vh = HEAD_DIM + BF16_SUBLANES
    cols = GQA * blk
    kw = HEAD_DIM + nbp
    return pl.pallas_call(
        _moba_kernel,
        grid=(bsz, D_KV, nb),
        in_specs=[pl.BlockSpec((None, blk, qgw), lambda b, h, i: (b, i, qcol0 + h)),
                  pl.BlockSpec((None, None, l, kw), lambda b, h, i: (b, h, 0, 0)),
                  pl.BlockSpec((None, None, nb, vh, blk), lambda b, h, i: (b, h, 0, 0, 0)),
                  pl.BlockSpec((None, nb, kvw), lambda b, h, i: (b, 0, 0))],
        out_specs=pl.BlockSpec((None, blk, qgw), lambda b, h, i: (b, i, h)),
        out_shape=jax.ShapeDtypeStruct((bsz, l, D_HEADS * HEAD_DIM), F32),
        scratch_shapes=[pltpu.VMEM((kw, cols), BF16), pltpu.VMEM((vh, cols), F32),
                        pltpu.VMEM((blk, cols), F32), pltpu.VMEM((blk, cols), F32)],
        compiler_params=_cparams("parallel", "arbitrary", "arbitrary"),
        name="moba",
    )(proj, k_aug, vt_bf, kmean)


ROUTER_LANES = 128


def _moe_gates(x, wr_ref, br_ref):
    x_hi, x_lo = _split_bf16(x)
    w_hi, w_lo = _split_bf16(wr_ref[...])
    logits = _dot(x_hi, w_hi) + _dot(x_lo, w_hi) + _dot(x_hi, w_lo) + br_ref[...]
    lane = lax.broadcasted_iota(jnp.int32, logits.shape, 1).astype(F32)
    big = float(ROUTER_LANES)
    gl = jnp.where(lane < N_GROUPS, logits, -jnp.inf)
    gmax = jnp.max(gl, axis=-1, keepdims=True)
    g_top = 1.0 / jnp.sum(jnp.exp(gl - gmax), axis=-1, keepdims=True)
    g_idx = jnp.min(jnp.where(gl == gmax, lane, big), axis=-1, keepdims=True)
    lo = N_GROUPS + EXP_PER_GROUP * g_idx
    el = jnp.where((lane >= lo) & (lane < lo + EXP_PER_GROUP), logits, -jnp.inf)
    m1 = jnp.max(el, axis=-1, keepdims=True)
    i1 = jnp.min(jnp.where(el == m1, lane, big), axis=-1, keepdims=True)
    el2 = jnp.where(lane == i1, -jnp.inf, el)
    m2 = jnp.max(el2, axis=-1, keepdims=True)
    i2 = jnp.min(jnp.where(el2 == m2, lane, big), axis=-1, keepdims=True)
    e21 = jnp.exp(m2 - m1)
    w1 = g_top / (1.0 + e21)
    w2 = w1 * e21
    return jnp.where(lane == i1, w1, 0.0) + jnp.where(lane == i2, w2, 0.0)


def _moe_kernel(x_ref, wr_ref, br_ref, wgu_ref, wd_ref, g_ref, b_ref, o_ref, gates_ref, xb_ref, acc_ref, *, alpha):
    e = pl.program_id(1)

    @pl.when(e == 0)
    def _():
        x = x_ref[...]
        gates_ref[...] = _moe_gates(x, wr_ref, br_ref)
        xb_ref[...] = x.astype(BF16)
        acc_ref[...] = jnp.zeros_like(acc_ref)

    gates = gates_ref[...]
    lane = lax.broadcasted_iota(jnp.int32, gates.shape, 1)
    gate_e = jnp.sum(jnp.where(lane == N_GROUPS + e, gates, 0.0), axis=-1, keepdims=True)
    hu = _dot(xb_ref[...], wgu_ref[...])
    h = _silu(hu[:, :EXP_HIDDEN]) * hu[:, EXP_HIDDEN:] * gate_e
    acc_ref[...] += _dot(h.astype(BF16), wd_ref[...])

    @pl.when(e == N_EXPERTS - 1)
    def _():
        o_ref[...] = _layer_norm_rows(alpha * x_ref[...] + acc_ref[...], g_ref[...], b_ref[...])


def _moe_ln(x2d, w_group, b_group, w_expert, b_expert, wgu_bf, wd_bf, g, b, alpha, tm=1024):
    t, d = x2d.shape
    tm = min(tm, t)
    pad = ROUTER_LANES - N_GROUPS - N_EXPERTS
    w_router = jnp.concatenate([w_group, w_expert, jnp.zeros((d, pad), F32)], axis=1).astype(F32)
    b_router = jnp.concatenate([b_group, b_expert, jnp.zeros((pad,), F32)]).astype(F32).reshape(1, ROUTER_LANES)
    return pl.pallas_call(
        functools.partial(_moe_kernel, alpha=alpha),
        grid=(t // tm, N_EXPERTS),
        in_specs=[pl.BlockSpec((tm, d), lambda i, e: (i, 0)),
                  pl.BlockSpec((d, ROUTER_LANES), lambda i, e: (0, 0)),
                  pl.BlockSpec((1, ROUTER_LANES), lambda i, e: (0, 0)),
                  pl.BlockSpec((None, d, 2 * EXP_HIDDEN), lambda i, e: (e, 0, 0)),
                  pl.BlockSpec((None, EXP_HIDDEN, d), lambda i, e: (e, 0, 0)),
                  pl.BlockSpec((1, d), lambda i, e: (0, 0)),
                  pl.BlockSpec((1, d), lambda i, e: (0, 0))],
        out_specs=pl.BlockSpec((tm, d), lambda i, e: (i, 0)),
        out_shape=jax.ShapeDtypeStruct((t, d), F32),
        scratch_shapes=[pltpu.VMEM((tm, ROUTER_LANES), F32), pltpu.VMEM((tm, d), BF16),
                        pltpu.VMEM((tm, d), F32)],
        compiler_params=_cparams("parallel", "arbitrary"),
        name="moe_ln",
    )(x2d, w_router, b_router, wgu_bf, wd_bf, g.reshape(1, d), b.reshape(1, d))


def kernel(x, hgrn_lb_logits, ev_w_in, ev_a_norm, ev_s5_a_re, ev_s5_a_im, ev_s5_log_dt, ev_s5_b_re, ev_s5_b_im, ev_s5_c_re, ev_s5_c_im, ev_s5_d, ev_s5_w_glu, ev_w_out, od_w_in, od_sinks, od_w_out, ln1_g, ln1_b, moe_w_group, moe_b_group, moe_w_expert, moe_b_expert, moe_w_gate_up, moe_w_down, ln2_g, ln2_b):
    bsz, l, d = x.shape
    depth = ln1_g.shape[0]
    alpha = (2.0 * depth) ** 0.25
    t = bsz * l
    lower_bounds = jnp.cumsum(jax.nn.softmax(hgrn_lb_logits.astype(F32), axis=0), axis=0)
    x2d = x.reshape(t, d)
    for layer in range(depth):
        j = layer // 2
        if layer % 2 == 0:
            proj = _proj(x2d, ev_w_in[j].astype(BF16))
            proj3 = proj.reshape(bsz, l, proj.shape[1])
            ya = _hgrn2(proj3, lower_bounds[layer], ev_a_norm[j])
            ycore = _s5_core(proj3[:, :, 4 * A_W:], ev_s5_a_re[j], ev_s5_a_im[j], ev_s5_log_dt[j],
                             ev_s5_b_re[j], ev_s5_b_im[j], ev_s5_c_re[j], ev_s5_c_im[j])
            yb = _s5_glu(ycore.reshape(t, B_WIDTH), proj, ev_s5_d[j], ev_s5_w_glu[j].astype(BF16))
            x2d = _mix_ln(ya.reshape(t, A_W), yb, x2d, ev_w_out[j].astype(BF16), ln1_g[layer], ln1_b[layer], alpha)
        else:
            proj = _proj(x2d, od_w_in[j].astype(BF16))
            proj3 = proj.reshape(bsz, l, proj.shape[1])
            yc = _swa(proj3, od_sinks[j])
            yd = _moba(proj3)
            x2d = _mix_ln(yc.reshape(t, -1), yd.reshape(t, -1), x2d, od_w_out[j].astype(BF16),
                          ln1_g[layer], ln1_b[layer], alpha)
        x2d = _moe_ln(x2d, moe_w_group[layer], moe_b_group[layer], moe_w_expert[layer], moe_b_expert[layer],
                      moe_w_gate_up[layer].astype(BF16), moe_w_down[layer].astype(BF16),
                      ln2_g[layer], ln2_b[layer], alpha)
    return x2d.reshape(bsz, l, d)
```

```python
import functools
import math

import jax
import jax.numpy as jnp
from jax import lax
from jax.experimental import pallas as pl
from jax.experimental.pallas import tpu as pltpu

F32 = jnp.float32
BF16 = jnp.bfloat16
NEG = -1e30
LN_EPS = 1e-5
RMS_EPS = 1e-6

A_HEADS = 4
A_DK = 128
A_DV = 128
A_CHUNK = 64
A_W = A_HEADS * A_DK

B_WIDTH = 512
B_GROUP = 16
B_GROUPS = B_WIDTH // B_GROUP
B_STATE = 64
S5_LC = 16
S5_CW = S5_LC * B_GROUP
S5_SLAB_LANES = 128

HEAD_DIM = 64
C_HEADS = 8
C_KV = 2
WINDOW = 128
D_HEADS = 8
D_KV = 2
MOBA_BLOCK = 256
MOBA_TOPK = 3
LOG2E = math.log2(math.e)
BF16_SUBLANES = 16
GQA = C_HEADS // C_KV

N_GROUPS = 4
EXP_PER_GROUP = 4
N_EXPERTS = 16
EXP_HIDDEN = 256

VMEM_LIMIT_BYTES = 48 * 1024 * 1024


def _cparams(*sem):
    return pltpu.CompilerParams(dimension_semantics=sem, vmem_limit_bytes=VMEM_LIMIT_BYTES)


def _nt_dot(a, b):
    return lax.dot_general(a, b, (((1,), (1,)), ((), ())), preferred_element_type=F32)


def _dot(a, b):
    return jnp.dot(a, b, preferred_element_type=F32)


def _split_bf16(v):
    hi = v.astype(BF16)
    lo = (v - hi.astype(F32)).astype(BF16)
    return hi, lo


def _silu(v):
    return v * jax.nn.sigmoid(v)


def _layer_norm_rows(r, g, b):
    mu = jnp.mean(r, axis=-1, keepdims=True)
    d = r - mu
    var = jnp.mean(d * d, axis=-1, keepdims=True)
    return d * lax.rsqrt(var + LN_EPS) * g + b


def _proj_kernel(x_ref, w_ref, o_ref):
    o_ref[...] = _dot(x_ref[...].astype(BF16), w_ref[...])


def _proj(x2d, w_bf, tm=512):
    t, d = x2d.shape
    n = w_bf.shape[1]
    return pl.pallas_call(
        _proj_kernel,
        grid=(t // tm,),
        in_specs=[pl.BlockSpec((tm, d), lambda i: (i, 0)),
                  pl.BlockSpec((d, n), lambda i: (0, 0))],
        out_specs=pl.BlockSpec((tm, n), lambda i: (i, 0)),
        out_shape=jax.ShapeDtypeStruct((t, n), F32),
        compiler_params=_cparams("parallel"),
        name="in_proj",
    )(x2d, w_bf)


def _mix_ln_kernel(ya_ref, yb_ref, x_ref, w_ref, g_ref, b_ref, o_ref, *, alpha):
    half = ya_ref.shape[1]
    mix = _dot(ya_ref[...].astype(BF16), w_ref[0:half, :]) + _dot(yb_ref[...].astype(BF16), w_ref[half:, :])
    o_ref[...] = _layer_norm_rows(alpha * x_ref[...] + mix, g_ref[...], b_ref[...])


def _mix_ln(ya, yb, x2d, w_out_bf, g, b, alpha, tm=512):
    t, d = x2d.shape
    half = ya.shape[1]
    return pl.pallas_call(
        functools.partial(_mix_ln_kernel, alpha=alpha),
        grid=(t // tm,),
        in_specs=[pl.BlockSpec((tm, half), lambda i: (i, 0)),
                  pl.BlockSpec((tm, half), lambda i: (i, 0)),
                  pl.BlockSpec((tm, d), lambda i: (i, 0)),
                  pl.BlockSpec((2 * half, d), lambda i: (0, 0)),
                  pl.BlockSpec((1, d), lambda i: (0, 0)),
                  pl.BlockSpec((1, d), lambda i: (0, 0))],
        out_specs=pl.BlockSpec((tm, d), lambda i: (i, 0)),
        out_shape=jax.ShapeDtypeStruct((t, d), F32),
        compiler_params=_cparams("parallel"),
        name="mix_ln",
    )(ya, yb, x2d, w_out_bf, g.reshape(1, d), b.reshape(1, d))


def _hgrn2_kernel(proj_ref, lb_ref, ng_ref, tri_ref, o_ref, st_ref, *, tb):
    @pl.when(pl.program_id(1) == 0)
    def _():
        st_ref[...] = jnp.zeros_like(st_ref)

    ql = proj_ref[:, 0:A_W]
    fl = proj_ref[:, A_W:2 * A_W]
    iv = proj_ref[:, 2 * A_W:3 * A_W]
    gt = proj_ref[:, 3 * A_W:4 * A_W]
    lb = lb_ref[...]
    f = lb + (1.0 - lb) * jax.nn.sigmoid(fl)
    k = 1.0 - f
    lf = jnp.log(f)
    lf_hi, lf_lo = _split_bf16(lf)
    tri = tri_ref[...]
    b = _dot(tri, lf_hi) + _dot(tri, lf_lo)
    qd = _silu(ql) * jnp.exp(b)
    ki = k * jnp.exp(-b)
    og = ng_ref[...] * _silu(gt)

    row = lax.broadcasted_iota(jnp.int32, (A_CHUNK, A_CHUNK), 0)
    col = lax.broadcasted_iota(jnp.int32, (A_CHUNK, A_CHUNK), 1)
    causal = col <= row
    v_t = [iv[:, h * A_DV:(h + 1) * A_DV].T for h in range(A_HEADS)]

    for c in range(tb // A_CHUNK):
        r0 = c * A_CHUNK
        b_c = b[r0:r0 + A_CHUNK, :]
        b_last = b[r0 + A_CHUNK - 1:r0 + A_CHUNK, :]
        kt = k[r0:r0 + A_CHUNK, :] * jnp.exp(b_last - b_c)
        dec = jnp.exp(b_last)
        for h in range(A_HEADS):
            ls = slice(h * A_DK, (h + 1) * A_DK)
            qd_h = qd[r0:r0 + A_CHUNK, ls].astype(BF16)
            ki_h = ki[r0:r0 + A_CHUNK, ls].astype(BF16)
            v_h = iv[r0:r0 + A_CHUNK, ls]
            att = jnp.where(causal, _nt_dot(qd_h, ki_h), 0.0)
            s_t = st_ref[h]
            o = _dot(att.astype(BF16), v_h.astype(BF16)) + _nt_dot(qd_h, s_t.astype(BF16))
            st_ref[h] = s_t * dec[:, ls] + _dot(v_t[h][:, r0:r0 + A_CHUNK].astype(BF16), kt[:, ls].astype(BF16))
            o = o * lax.rsqrt(jnp.mean(o * o, axis=-1, keepdims=True) + RMS_EPS)
            o_ref[r0:r0 + A_CHUNK, ls] = o * og[r0:r0 + A_CHUNK, ls]


def _hgrn2(proj, lb, norm_g, tb=256):
    bsz, l, _ = proj.shape
    tb = min(tb, l)
    idx = jnp.arange(tb)
    tri = ((idx[:, None] >= idx[None, :]) & (idx[:, None] // A_CHUNK == idx[None, :] // A_CHUNK)).astype(BF16)
    return pl.pallas_call(
        functools.partial(_hgrn2_kernel, tb=tb),
        grid=(bsz, l // tb),
        in_specs=[pl.BlockSpec((None, tb, 4 * A_W), lambda b, t: (b, t, 0)),
                  pl.BlockSpec((1, A_W), lambda b, t: (0, 0)),
                  pl.BlockSpec((1, A_W), lambda b, t: (0, 0)),
                  pl.BlockSpec((tb, tb), lambda b, t: (0, 0))],
        out_specs=pl.BlockSpec((None, tb, A_W), lambda b, t: (b, t, 0)),
        out_shape=jax.ShapeDtypeStruct((bsz, l, A_W), F32),
        scratch_shapes=[pltpu.VMEM((A_HEADS, A_DV, A_DK), F32)],
        compiler_params=_cparams("parallel", "arbitrary"),
        name="hgrn2",
    )(proj, lb.reshape(1, A_W), norm_g.reshape(1, A_W), tri)


def _s5_weights(a_re, a_im, log_dt, b_re, b_im, c_re, c_im):
    g, p = a_re.shape
    lc = S5_LC
    dt = jnp.exp(log_dt.astype(F32))[:, None]
    ar, ai = a_re.astype(F32), a_im.astype(F32)
    mag = jnp.exp(dt * ar)
    abar_re, abar_im = mag * jnp.cos(dt * ai), mag * jnp.sin(dt * ai)
    den = ar * ar + ai * ai
    xr, xi = abar_re - 1.0, abar_im
    fr = (xr * ar + xi * ai) / den
    fi = (xi * ar - xr * ai) / den
    br, bi = b_re.astype(F32), b_im.astype(F32)
    bb_re = fr[..., None] * br - fi[..., None] * bi
    bb_im = fr[..., None] * bi + fi[..., None] * br
    cr, ci = c_re.astype(F32), c_im.astype(F32)
    tau = jnp.arange(lc + 1, dtype=F32)[:, None, None]
    pmag = jnp.exp(tau * (dt * ar)[None])
    pw_re = pmag * jnp.cos(tau * (dt * ai)[None])
    pw_im = pmag * jnp.sin(tau * (dt * ai)[None])
    hp = lax.Precision.HIGHEST
    ab_re = pw_re[..., None] * bb_re[None] - pw_im[..., None] * bb_im[None]
    ab_im = pw_re[..., None] * bb_im[None] + pw_im[..., None] * bb_re[None]
    kk = (jnp.einsum('gnp,tgpm->tgnm', cr, ab_re, precision=hp)
          - jnp.einsum('gnp,tgpm->tgnm', ci, ab_im, precision=hp))
    s_idx = jnp.arange(lc)[:, None]
    t_idx = jnp.arange(lc)[None, :]
    lag = t_idx - s_idx
    toe = jnp.where((lag >= 0)[:, :, None, None, None], kk[jnp.clip(lag, 0, lc)], 0.0)
    toe = toe.transpose(2, 0, 4, 1, 3).reshape(g, lc * B_GROUP, lc * B_GROUP)
    rev = lc - 1 - jnp.arange(lc)
    wx_re = ab_re[rev].transpose(1, 0, 3, 2).reshape(g, lc * B_GROUP, p)
    wx_im = ab_im[rev].transpose(1, 0, 3, 2).reshape(g, lc * B_GROUP, p)
    pr, pi_ = pw_re[1:], pw_im[1:]
    wy_re = cr[None] * pr[:, :, None, :] - ci[None] * pi_[:, :, None, :]
    wy_im = -(cr[None] * pi_[:, :, None, :] + ci[None] * pr[:, :, None, :])
    wy_re = wy_re.transpose(1, 3, 0, 2).reshape(g, p, lc * B_GROUP)
    wy_im = wy_im.transpose(1, 3, 0, 2).reshape(g, p, lc * B_GROUP)
    return toe, wx_re, wx_im, wy_re, wy_im, pw_re[lc], pw_im[lc]


def _pair_pack_rows(w):
    g, r, c = w.shape
    w = w.reshape(g // 2, 2, r, c)
    z = jnp.zeros((g // 2, r, c), w.dtype)
    top = jnp.concatenate([w[:, 0], z], axis=2)
    bot = jnp.concatenate([z, w[:, 1]], axis=2)
    return jnp.concatenate([top, bot], axis=1)


def _s5_inject_kernel(u_ref, wre_ref, wim_ref, xre_ref, xim_ref):
    lhs = jnp.concatenate([u_ref[0], u_ref[1]], axis=1)
    xre_ref[...] = _dot(lhs, wre_ref[...])
    xim_ref[...] = _dot(lhs, wim_ref[...])


def _s5_scan_kernel(xre_ref, xim_ref, are_ref, aim_ref, hre_ref, him_ref, sre_ref, sim_ref, *, tcs):
    @pl.when(pl.program_id(0) == 0)
    def _():
        sre_ref[...] = jnp.zeros_like(sre_ref)
        sim_ref[...] = jnp.zeros_like(sim_ref)

    ar = are_ref[...]
    ai = aim_ref[...]

    def body(r, carry):
        hr, hi = carry
        hre_ref[r] = hr
        him_ref[r] = hi
        return ar * hr - ai * hi + xre_ref[r], ar * hi + ai * hr + xim_ref[r]

    hr, hi = lax.fori_loop(0, tcs, body, (sre_ref[...], sim_ref[...]), unroll=8)
    sre_ref[...] = hr
    sim_ref[...] = hi


def _s5_readout_kernel(u_ref, toe_ref, wyre_ref, wyim_ref, hre_ref, him_ref, y_ref):
    hre = hre_ref[...].astype(BF16)
    him = him_ref[...].astype(BF16)
    for r in range(2):
        y_ref[r] = (_dot(u_ref[r], toe_ref[r])
                    + _dot(hre, wyre_ref[r]) + _dot(him, wyim_ref[r]))


def _s5_pack_kernel(*refs):
    u_refs, o_ref = refs[:-1], refs[-1]
    tc = o_ref.shape[1]
    gps = S5_SLAB_LANES // B_GROUP
    for r, u_ref in enumerate(u_refs):
        rows = [u_ref[pl.ds(s, tc, stride=S5_LC), :] for s in range(S5_LC)]
        for j in range(gps):
            piece = jnp.concatenate([row[:, j * B_GROUP:(j + 1) * B_GROUP] for row in rows], axis=1)
            o_ref[r * gps + j] = piece.astype(BF16)


def _s5_pack(proj2d, tc=128):
    t, wd = proj2d.shape
    nc = t // S5_LC
    tc = min(tc, nc)
    col0 = (wd - B_WIDTH) // S5_SLAB_LANES
    nslab = B_WIDTH // S5_SLAB_LANES
    return pl.pallas_call(
        _s5_pack_kernel,
        grid=(nc // tc,),
        in_specs=[pl.BlockSpec((tc * S5_LC, S5_SLAB_LANES), functools.partial(lambda i, r: (i, col0 + r), r=r))
                  for r in range(nslab)],
        out_specs=pl.BlockSpec((B_GROUPS, tc, S5_CW), lambda i: (0, i, 0)),
        out_shape=jax.ShapeDtypeStruct((B_GROUPS, nc, S5_CW), BF16),
        compiler_params=_cparams("parallel"),
        name="s5_pack",
    )(*([proj2d] * nslab))


def _s5_core(proj2d, bsz, a_re, a_im, log_dt, b_re, b_im, c_re, c_im):
    l = proj2d.shape[0] // bsz
    g, p, lc, cw = B_GROUPS, B_STATE, S5_LC, S5_CW
    ncb = l // lc
    toe, wx_re, wx_im, wy_re, wy_im, a_re_lc, a_im_lc = _s5_weights(a_re, a_im, log_dt, b_re, b_im, c_re, c_im)
    toe = toe.astype(BF16)
    wx_re_p = _pair_pack_rows(wx_re).astype(BF16)
    wx_im_p = _pair_pack_rows(wx_im).astype(BF16)
    zero = jnp.zeros_like(wy_re)
    even = (jnp.arange(g) % 2 == 0)[:, None, None]
    wy_re_p = jnp.where(even, jnp.concatenate([wy_re, zero], 1), jnp.concatenate([zero, wy_re], 1)).astype(BF16)
    wy_im_p = jnp.where(even, jnp.concatenate([wy_im, zero], 1), jnp.concatenate([zero, wy_im], 1)).astype(BF16)

    u2 = _s5_pack(proj2d)
    tc = min(1024, ncb)
    nct = ncb // tc
    slab = 2 * p
    x_shape = jax.ShapeDtypeStruct((ncb, bsz * g * p), F32)
    x_spec = pl.BlockSpec((tc, slab), lambda j, b, i: (i, b * (g // 2) + j))
    u_spec = pl.BlockSpec((2, tc, cw), lambda j, b, i: (j, b * nct + i, 0))
    xre, xim = pl.pallas_call(
        _s5_inject_kernel,
        grid=(g // 2, bsz, nct),
        in_specs=[u_spec,
                  pl.BlockSpec((None, 2 * cw, slab), lambda j, b, i: (j, 0, 0)),
                  pl.BlockSpec((None, 2 * cw, slab), lambda j, b, i: (j, 0, 0))],
        out_specs=[x_spec, x_spec],
        out_shape=[x_shape, x_shape],
        compiler_params=_cparams("parallel", "parallel", "parallel"),
        name="s5_inject",
    )(u2, wx_re_p, wx_im_p)

    lanes = 512
    rows = bsz * g * p // lanes
    tcs = min(128, ncb)
    a_tile = lambda a: jnp.tile(a.reshape(g * p // lanes, lanes), (bsz, 1))
    seq_spec = pl.BlockSpec((tcs, rows, lanes), lambda i: (i, 0, 0))
    par_spec = pl.BlockSpec((rows, lanes), lambda i: (0, 0))
    seq_shape = jax.ShapeDtypeStruct((ncb, rows, lanes), F32)
    hre, him = pl.pallas_call(
        functools.partial(_s5_scan_kernel, tcs=tcs),
        grid=(ncb // tcs,),
        in_specs=[seq_spec, seq_spec, par_spec, par_spec],
        out_specs=[seq_spec, seq_spec],
        out_shape=[seq_shape, seq_shape],
        scratch_shapes=[pltpu.VMEM((rows, lanes), F32), pltpu.VMEM((rows, lanes), F32)],
        compiler_params=_cparams("arbitrary"),
        name="s5_scan",
    )(xre.reshape(ncb, rows, lanes), xim.reshape(ncb, rows, lanes), a_tile(a_re_lc), a_tile(a_im_lc))
    hre = hre.reshape(ncb, bsz * g * p)
    him = him.reshape(ncb, bsz * g * p)

    y2 = pl.pallas_call(
        _s5_readout_kernel,
        grid=(g // 2, bsz, nct),
        in_specs=[u_spec,
                  pl.BlockSpec((2, cw, cw), lambda j, b, i: (j, 0, 0)),
                  pl.BlockSpec((2, slab, cw), lambda j, b, i: (j, 0, 0)),
                  pl.BlockSpec((2, slab, cw), lambda j, b, i: (j, 0, 0)),
                  x_spec, x_spec],
        out_specs=u_spec,
        out_shape=jax.ShapeDtypeStruct((g, bsz * ncb, cw), F32),
        compiler_params=_cparams("parallel", "parallel", "parallel"),
        name="s5_readout",
    )(u2, toe, wy_re_p, wy_im_p, hre, him)
    return y2


def _s5_glu_kernel(y2_ref, u_ref, d_ref, w_ref, o_ref, yc_ref):
    tc = y2_ref.shape[1]
    gps = S5_SLAB_LANES // B_GROUP
    for r in range(yc_ref.shape[0]):
        grp = [y2_ref[r * gps + j] for j in range(gps)]
        for s in range(S5_LC):
            yc_ref[r, pl.ds(s, tc, stride=S5_LC), :] = jnp.concatenate(
                [blk[:, s * B_GROUP:(s + 1) * B_GROUP] for blk in grp], axis=1)
    yc = jnp.concatenate([yc_ref[r] for r in range(yc_ref.shape[0])], axis=1)
    y = jax.nn.gelu(yc + d_ref[...] * u_ref[...])
    z = _dot(y.astype(BF16), w_ref[...])
    o_ref[...] = z[:, :B_WIDTH] * jax.nn.sigmoid(z[:, B_WIDTH:])


def _s5_glu(y2, proj2d, d_skip, w_glu_bf, tc=64):
    t = proj2d.shape[0]
    nc = t // S5_LC
    tc = min(tc, nc)
    tm = tc * S5_LC
    ucol = proj2d.shape[1] // B_WIDTH - 1
    return pl.pallas_call(
        _s5_glu_kernel,
        grid=(nc // tc,),
        in_specs=[pl.BlockSpec((B_GROUPS, tc, S5_CW), lambda i: (0, i, 0)),
                  pl.BlockSpec((tm, B_WIDTH), lambda i: (i, ucol)),
                  pl.BlockSpec((1, B_WIDTH), lambda i: (0, 0)),
                  pl.BlockSpec((B_WIDTH, 2 * B_WIDTH), lambda i: (0, 0))],
        out_specs=pl.BlockSpec((tm, B_WIDTH), lambda i: (i, 0)),
        out_shape=jax.ShapeDtypeStruct((t, B_WIDTH), F32),
        scratch_shapes=[pltpu.VMEM((B_WIDTH // S5_SLAB_LANES, tm, S5_SLAB_LANES), F32)],
        compiler_params=_cparams("parallel"),
        name="s5_glu",
    )(y2, proj2d, d_skip.reshape(1, B_WIDTH), w_glu_bf)


def _swa_kernel(sink_ref, q_ref, kp_ref, kc_ref, vp_ref, vc_ref, o_ref):
    first = pl.program_id(1) == 0
    w = WINDOW
    rows = GQA * w
    qpos = lax.broadcasted_iota(jnp.int32, (rows, 2 * w), 0) % w + w
    kpos = lax.broadcasted_iota(jnp.int32, (rows, 2 * w), 1)
    valid = (kpos <= qpos) & (qpos - kpos < w) & jnp.logical_not(first & (kpos < w))
    rowid = lax.broadcasted_iota(jnp.int32, (rows, 1), 0) // w
    scale = HEAD_DIM ** -0.5
    outs = []
    for kv in range(C_KV):
        ls = slice(kv * HEAD_DIM, (kv + 1) * HEAD_DIM)
        k2 = jnp.concatenate([kp_ref[:, ls], kc_ref[:, ls]], axis=0).astype(BF16)
        v2 = jnp.concatenate([vp_ref[:, ls], vc_ref[:, ls]], axis=0).astype(BF16)
        q4 = jnp.concatenate(
            [q_ref[:, (kv * GQA + g) * HEAD_DIM:(kv * GQA + g + 1) * HEAD_DIM] for g in range(GQA)], axis=0)
        s = _nt_dot((q4 * scale).astype(BF16), k2)
        s = jnp.where(valid, s, NEG)
        sink = jnp.zeros((rows, 1), F32)
        for g in range(GQA):
            sink = jnp.where(rowid == g, sink_ref[kv * GQA + g], sink)
        m = jnp.maximum(jnp.max(s, axis=-1, keepdims=True), sink)
        p = jnp.exp(s - m)
        denom = jnp.sum(p, axis=-1, keepdims=True) + jnp.exp(sink - m)
        o = _dot(p.astype(BF16), v2) / denom
        outs.extend(o[g * w:(g + 1) * w, :] for g in range(GQA))
    o_ref[...] = jnp.concatenate(outs, axis=1)


def _swa(proj, sinks):
    bsz, l, _ = proj.shape
    w = WINDOW
    qw = C_HEADS * HEAD_DIM
    kvw = C_KV * HEAD_DIM
    kcol = qw // kvw
    prev = lambda b, n: (b, jnp.maximum(n - 1, 0), kcol)
    prev_v = lambda b, n: (b, jnp.maximum(n - 1, 0), kcol + 1)
    return pl.pallas_call(
        _swa_kernel,
        grid=(bsz, l // w),
        in_specs=[pl.BlockSpec(memory_space=pltpu.SMEM),
                  pl.BlockSpec((None, w, qw), lambda b, n: (b, n, 0)),
                  pl.BlockSpec((None, w, kvw), prev),
                  pl.BlockSpec((None, w, kvw), lambda b, n: (b, n, kcol)),
                  pl.BlockSpec((None, w, kvw), prev_v),
                  pl.BlockSpec((None, w, kvw), lambda b, n: (b, n, kcol + 1))],
        out_specs=pl.BlockSpec((None, w, qw), lambda b, n: (b, n, 0)),
        out_shape=jax.ShapeDtypeStruct((bsz, l, qw), F32),
        compiler_params=_cparams("parallel", "arbitrary"),
        name="swa",
    )(sinks.astype(F32), proj, proj, proj, proj, proj)


def _kmean_kernel(k_ref, o_ref):
    nb = o_ref.shape[0]
    o_ref[...] = jnp.mean(k_ref[...].reshape(nb, MOBA_BLOCK, k_ref.shape[-1]), axis=1)


def _moba_kernel(q_ref, k_ref, vt_ref, km_ref, o_ref, qaug_ref, acc_ref, sa_ref, sb_ref):
    kvh = pl.program_id(1)
    i = pl.program_id(2)
    blk = MOBA_BLOCK
    cols = GQA * blk
    nb = km_ref.shape[0]
    scale = HEAD_DIM ** -0.5
    qt = q_ref[...].T * scale
    q_t = jnp.concatenate([qt[g * HEAD_DIM:(g + 1) * HEAD_DIM, :] for g in range(GQA)], axis=1)
    q_hi, q_lo = _split_bf16(q_t)

    km = jnp.where(kvh == 0, km_ref[:, :HEAD_DIM], km_ref[:, HEAD_DIM:])
    km_hi, km_lo = _split_bf16(km)
    gate = _dot(km_hi, q_hi) + _dot(km_hi, q_lo) + _dot(km_lo, q_hi)
    blkid = lax.broadcasted_iota(jnp.int32, (nb, cols), 0).astype(F32)
    past = blkid < i.astype(F32)
    gate = jnp.where(past, gate, NEG)
    sel = jnp.zeros((nb, cols), F32)
    for _ in range(min(MOBA_TOPK, nb)):
        top = jnp.max(gate, axis=0, keepdims=True)
        first = jnp.min(jnp.where(gate == top, blkid, float(nb)), axis=0, keepdims=True)
        hit = blkid == first
        sel = jnp.where(hit & past, 1.0, sel)
        gate = jnp.where(hit, -jnp.inf, gate)
    q2 = (q_t * LOG2E).astype(BF16)
    qaug_ref[0:HEAD_DIM, :] = q2
    pad = qaug_ref.shape[0] - HEAD_DIM - nb
    mask = jnp.where(sel > 0.0, 0.0, NEG)
    if pad:
        mask = jnp.concatenate([mask, jnp.zeros((pad, cols), F32)], axis=0)
    qaug_ref[HEAD_DIM:, :] = mask.astype(BF16)

    def k_rows(j):
        return k_ref[pl.ds(pl.multiple_of(j * blk, blk), blk), :]

    def scores(j):
        return _dot(k_rows(j), qaug_ref[...])

    s = _dot(k_rows(i)[:, :HEAD_DIM], q2)
    kpos = lax.broadcasted_iota(jnp.int32, (blk, cols), 0)
    qpos = lax.broadcasted_iota(jnp.int32, (blk, cols), 1) % blk
    s = jnp.where(kpos <= qpos, s, NEG)
    m0 = jnp.max(s, axis=0, keepdims=True)
    p = jnp.exp2(s - m0)
    acc_ref[...] = _dot(vt_ref[i], p.astype(BF16))

    def stage(s_ref, j):
        s_j = scores(j)
        s_ref[...] = s_j
        return jnp.max(s_j, axis=0, keepdims=True)

    def absorb(s_ref, s_max, j, m_old):
        m_new = jnp.maximum(m_old, s_max)
        p_j = jnp.exp2(s_ref[...] - m_new)
        corr = jnp.exp2(m_old - m_new)
        acc_ref[...] = acc_ref[...] * corr + _dot(vt_ref[j], p_j.astype(BF16))
        return m_new

    def body(t, carry):
        m, max_a = carry
        j0 = 2 * t
        max_b = stage(sb_ref, j0 + 1)
        m = absorb(sa_ref, max_a, j0, m)
        max_a = stage(sa_ref, jnp.minimum(j0 + 2, nb - 1))
        m = absorb(sb_ref, max_b, j0 + 1, m)
        return m, max_a

    lax.fori_loop(0, (i + 1) // 2, body, (m0, stage(sa_ref, 0)))
    out_t = acc_ref[0:HEAD_DIM, :] / acc_ref[HEAD_DIM:HEAD_DIM + 1, :]
    stacked = jnp.concatenate([out_t[:, g * blk:(g + 1) * blk] for g in range(GQA)], axis=0)
    o_ref[...] = stacked.T


def _moba(proj):
    bsz, l, _ = proj.shape
    blk = MOBA_BLOCK
    nb = l // blk
    kvw = D_KV * HEAD_DIM
    qgw = GQA * HEAD_DIM
    q0 = C_HEADS * HEAD_DIM + 2 * C_KV * HEAD_DIM
    k0 = q0 + D_HEADS * HEAD_DIM
    qcol0 = q0 // qgw
    kmean = pl.pallas_call(
        _kmean_kernel,
        grid=(bsz,),
        in_specs=[pl.BlockSpec((None, l, kvw), lambda b: (b, 0, k0 // kvw))],
        out_specs=pl.BlockSpec((None, nb, kvw), lambda b: (b, 0, 0)),
        out_shape=jax.ShapeDtypeStruct((bsz, nb, kvw), F32),
        compiler_params=_cparams("parallel"),
        name="moba_kmean",
    )(proj)
    nbp = -(-nb // BF16_SUBLANES) * BF16_SUBLANES
    k_bf = proj[:, :, k0:k0 + kvw].reshape(bsz, l, D_KV, HEAD_DIM).transpose(0, 2, 1, 3).astype(BF16)
    onehot = (jnp.arange(l)[:, None] // blk == jnp.arange(nbp)[None, :]).astype(BF16)
    k_aug = jnp.concatenate([k_bf, jnp.broadcast_to(onehot, (bsz, D_KV, l, nbp))], axis=-1)
    vt_bf = (proj[:, :, k0 + kvw:k0 + 2 * kvw].reshape(bsz, nb, blk, D_KV, HEAD_DIM)
             .transpose(0, 3, 1, 4, 2).astype(BF16))
    ones_rows = jnp.zeros((bsz, D_KV, nb, BF16_SUBLANES, blk), BF16).at[:, :, :, 0, :].set(1.0)
    vt_bf = jnp.concatenate([vt_bf, ones_rows], axis=3)
    vh = HEAD_DIM + BF16_SUBLANES
    cols = GQA * blk
    kw = HEAD_DIM + nbp
    return pl.pallas_call(
        _moba_kernel,
        grid=(bsz, D_KV, nb),
        in_specs=[pl.BlockSpec((None, blk, qgw), lambda b, h, i: (b, i, qcol0 + h)),
                  pl.BlockSpec((None, None, l, kw), lambda b, h, i: (b, h, 0, 0)),
                  pl.BlockSpec((None, None, nb, vh, blk), lambda b, h, i: (b, h, 0, 0, 0)),
                  pl.BlockSpec((None, nb, kvw), lambda b, h, i: (b, 0, 0))],
        out_specs=pl.BlockSpec((None, blk, qgw), lambda b, h, i: (b, i, h)),
        out_shape=jax.ShapeDtypeStruct((bsz, l, D_HEADS * HEAD_DIM), F32),
        scratch_shapes=[pltpu.VMEM((kw, cols), BF16), pltpu.VMEM((vh, cols), F32),
                        pltpu.VMEM((blk, cols), F32), pltpu.VMEM((blk, cols), F32)],
        compiler_params=_cparams("parallel", "arbitrary", "arbitrary"),
        name="moba",
    )(proj, k_aug, vt_bf, kmean)


ROUTER_LANES = 128


def _router_logits(x, wr_ref, br_ref):
    x_hi, x_lo = _split_bf16(x)
    w_hi, w_lo = _split_bf16(wr_ref[...])
    return _dot(x_hi, w_hi) + _dot(x_lo, w_hi) + _dot(x_hi, w_lo) + br_ref[...]


def _lane_ids(shape):
    return lax.broadcasted_iota(jnp.int32, shape, 1).astype(F32)


def _top_group(logits):
    lane = _lane_ids(logits.shape)
    gl = jnp.where(lane < N_GROUPS, logits, -jnp.inf)
    gmax = jnp.max(gl, axis=-1, keepdims=True)
    return jnp.min(jnp.where(gl == gmax, lane, float(ROUTER_LANES)), axis=-1, keepdims=True)


def _moe_gates(logits, g_idx):
    lane = _lane_ids(logits.shape)
    big = float(ROUTER_LANES)
    gl = jnp.where(lane < N_GROUPS, logits, -jnp.inf)
    gmax = jnp.max(gl, axis=-1, keepdims=True)
    g_logit = jnp.sum(jnp.where(lane == g_idx, logits, 0.0), axis=-1, keepdims=True)
    g_top = jnp.exp(g_logit - gmax) / jnp.sum(jnp.exp(gl - gmax), axis=-1, keepdims=True)
    lo = N_GROUPS + EXP_PER_GROUP * g_idx
    el = jnp.where((lane >= lo) & (lane < lo + EXP_PER_GROUP), logits, -jnp.inf)
    m1 = jnp.max(el, axis=-1, keepdims=True)
    i1 = jnp.min(jnp.where(el == m1, lane, big), axis=-1, keepdims=True)
    el2 = jnp.where(lane == i1, -jnp.inf, el)
    m2 = jnp.max(el2, axis=-1, keepdims=True)
    i2 = jnp.min(jnp.where(el2 == m2, lane, big), axis=-1, keepdims=True)
    e21 = jnp.exp(m2 - m1)
    w1 = g_top / (1.0 + e21)
    w2 = w1 * e21
    return jnp.where(lane == i1, w1, 0.0) + jnp.where(lane == i2, w2, 0.0)


def _moe_route_kernel(x_ref, wr_ref, br_ref, tri_ref, o_ref, cnt_ref):
    @pl.when(pl.program_id(0) == 0)
    def _():
        cnt_ref[...] = jnp.zeros_like(cnt_ref)

    logits = _router_logits(x_ref[...], wr_ref, br_ref)
    g_idx = _top_group(logits)
    lane = _lane_ids(logits.shape)
    onehot = jnp.where(lane == g_idx, 1.0, 0.0)
    before = _dot(tri_ref[...], onehot.astype(BF16)) + cnt_ref[...]
    rank = jnp.sum(onehot * before, axis=-1, keepdims=True)
    cnt_ref[...] += jnp.sum(onehot, axis=0, keepdims=True)
    o_ref[...] = jnp.where(lane == 0.0, g_idx, jnp.where(lane == 1.0, rank, 0.0))


def _gather_rows(idx_ref, base, src_hbm, dst_ref, sem, n):
    def start(r, c):
        pltpu.make_async_copy(src_hbm.at[pl.ds(idx_ref[base + r], 1)], dst_ref.at[pl.ds(r, 1)], sem).start()
        return c

    def wait(r, c):
        pltpu.make_async_copy(src_hbm.at[pl.ds(0, 1)], dst_ref.at[pl.ds(r, 1)], sem).wait()
        return c

    lax.fori_loop(0, n, start, 0, unroll=8)
    lax.fori_loop(0, n, wait, 0, unroll=8)


def _moe_gather_kernel(idx_ref, x_hbm, o_ref, sem):
    n = o_ref.shape[0]
    _gather_rows(idx_ref, pl.program_id(0) * n, x_hbm, o_ref, sem, n)


def _moe_expert_kernel(tg_ref, xs_ref, wr_ref, br_ref, wgu_ref, wd_ref, o_ref, gates_ref, xb_ref, acc_ref):
    gid = tg_ref[pl.program_id(0)]
    e = pl.program_id(1)

    @pl.when(e == 0)
    def _():
        x = xs_ref[...]
        gates_ref[...] = _moe_gates(_router_logits(x, wr_ref, br_ref), gid.astype(F32))
        xb_ref[...] = x.astype(BF16)
        acc_ref[...] = jnp.zeros_like(acc_ref)

    gates = gates_ref[...]
    lane = lax.broadcasted_iota(jnp.int32, gates.shape, 1)
    gate_e = jnp.sum(jnp.where(lane == N_GROUPS + EXP_PER_GROUP * gid + e, gates, 0.0), axis=-1, keepdims=True)
    hu = _dot(xb_ref[...], wgu_ref[...])
    h = _silu(hu[:, :EXP_HIDDEN]) * hu[:, EXP_HIDDEN:] * gate_e
    acc_ref[...] += _dot(h.astype(BF16), wd_ref[...])

    @pl.when(e == EXP_PER_GROUP - 1)
    def _():
        o_ref[...] = acc_ref[...]


def _moe_combine_kernel(idx_ref, ys_hbm, x_ref, g_ref, b_ref, o_ref, buf_ref, sem, *, alpha):
    n = o_ref.shape[0]
    _gather_rows(idx_ref, pl.program_id(0) * n, ys_hbm, buf_ref, sem, n)
    o_ref[...] = _layer_norm_rows(alpha * x_ref[...] + buf_ref[...], g_ref[...], b_ref[...])


def _moe_ln(x2d, w_group, b_group, w_expert, b_expert, wgu_bf, wd_bf, g, b, alpha, tm=1024):
    t, d = x2d.shape
    tm = min(tm, t)
    pad = ROUTER_LANES - N_GROUPS - N_EXPERTS
    w_router = jnp.concatenate([w_group, w_expert, jnp.zeros((d, pad), F32)], axis=1).astype(F32)
    b_router = jnp.concatenate([b_group, b_expert, jnp.zeros((pad,), F32)]).astype(F32).reshape(1, ROUTER_LANES)
    ids = jnp.arange(tm)
    tri = (ids[:, None] > ids[None, :]).astype(BF16)
    route = pl.pallas_call(
        _moe_route_kernel,
        grid=(t // tm,),
        in_specs=[pl.BlockSpec((tm, d), lambda i: (i, 0)),
                  pl.BlockSpec((d, ROUTER_LANES), lambda i: (0, 0)),
                  pl.BlockSpec((1, ROUTER_LANES), lambda i: (0, 0)),
                  pl.BlockSpec((tm, tm), lambda i: (0, 0))],
        out_specs=pl.BlockSpec((tm, ROUTER_LANES), lambda i: (i, 0)),
        out_shape=jax.ShapeDtypeStruct((t, ROUTER_LANES), F32),
        scratch_shapes=[pltpu.VMEM((1, ROUTER_LANES), F32)],
        compiler_params=_cparams("arbitrary"),
        name="moe_route",
    )(x2d, w_router, b_router, tri)

    gid = route[:, 0].astype(jnp.int32)
    rank = route[:, 1].astype(jnp.int32)
    counts = jnp.sum(gid[:, None] == jnp.arange(N_GROUPS)[None, :], axis=0)
    seg = (counts + tm - 1) // tm * tm
    ends = jnp.cumsum(seg)
    dest = (ends - seg)[gid] + rank
    n_tiles = t // tm + N_GROUPS
    m = n_tiles * tm
    src = jnp.zeros((m,), jnp.int32).at[dest].set(jnp.arange(t, dtype=jnp.int32))
    tile_gid = jnp.minimum(jnp.searchsorted(ends, jnp.arange(n_tiles) * tm, side="right"), N_GROUPS - 1).astype(jnp.int32)

    xs = pl.pallas_call(
        _moe_gather_kernel,
        grid_spec=pltpu.PrefetchScalarGridSpec(
            num_scalar_prefetch=1,
            grid=(n_tiles,),
            in_specs=[pl.BlockSpec(memory_space=pl.ANY)],
            out_specs=pl.BlockSpec((tm, d), lambda i, idx: (i, 0)),
            scratch_shapes=[pltpu.SemaphoreType.DMA(())]),
        out_shape=jax.ShapeDtypeStruct((m, d), F32),
        compiler_params=_cparams("arbitrary"),
        name="moe_gather",
    )(src, x2d)

    ys = pl.pallas_call(
        _moe_expert_kernel,
        grid_spec=pltpu.PrefetchScalarGridSpec(
            num_scalar_prefetch=1,
            grid=(n_tiles, EXP_PER_GROUP),
            in_specs=[pl.BlockSpec((tm, d), lambda i, e, tg: (i, 0)),
                      pl.BlockSpec((d, ROUTER_LANES), lambda i, e, tg: (0, 0)),
                      pl.BlockSpec((1, ROUTER_LANES), lambda i, e, tg: (0, 0)),
                      pl.BlockSpec((None, d, 2 * EXP_HIDDEN), lambda i, e, tg: (tg[i] * EXP_PER_GROUP + e, 0, 0)),
                      pl.BlockSpec((None, EXP_HIDDEN, d), lambda i, e, tg: (tg[i] * EXP_PER_GROUP + e, 0, 0))],
            out_specs=pl.BlockSpec((tm, d), lambda i, e, tg: (i, 0)),
            scratch_shapes=[pltpu.VMEM((tm, ROUTER_LANES), F32), pltpu.VMEM((tm, d), BF16),
                            pltpu.VMEM((tm, d), F32)]),
        out_shape=jax.ShapeDtypeStruct((m, d), F32),
        compiler_params=_cparams("arbitrary", "arbitrary"),
        name="moe_experts",
    )(tile_gid, xs, w_router, b_router, wgu_bf, wd_bf)

    return pl.pallas_call(
        functools.partial(_moe_combine_kernel, alpha=alpha),
        grid_spec=pltpu.PrefetchScalarGridSpec(
            num_scalar_prefetch=1,
            grid=(t // tm,),
            in_specs=[pl.BlockSpec(memory_space=pl.ANY),
                      pl.BlockSpec((tm, d), lambda i, idx: (i, 0)),
                      pl.BlockSpec((1, d), lambda i, idx: (0, 0)),
                      pl.BlockSpec((1, d), lambda i, idx: (0, 0))],
            out_specs=pl.BlockSpec((tm, d), lambda i, idx: (i, 0)),
            scratch_shapes=[pltpu.VMEM((tm, d), F32), pltpu.SemaphoreType.DMA(())]),
        out_shape=jax.ShapeDtypeStruct((t, d), F32),
        compiler_params=_cparams("arbitrary"),
        name="moe_combine",
    )(dest, ys, x2d, g.reshape(1, d), b.reshape(1, d))


def kernel(x, hgrn_lb_logits, ev_w_in, ev_a_norm, ev_s5_a_re, ev_s5_a_im, ev_s5_log_dt, ev_s5_b_re, ev_s5_b_im, ev_s5_c_re, ev_s5_c_im, ev_s5_d, ev_s5_w_glu, ev_w_out, od_w_in, od_sinks, od_w_out, ln1_g, ln1_b, moe_w_group, moe_b_group, moe_w_expert, moe_b_expert, moe_w_gate_up, moe_w_down, ln2_g, ln2_b):
    bsz, l, d = x.shape
    depth = ln1_g.shape[0]
    alpha = (2.0 * depth) ** 0.25
    t = bsz * l
    lower_bounds = jnp.cumsum(jax.nn.softmax(hgrn_lb_logits.astype(F32), axis=0), axis=0)
    x2d = x.reshape(t, d)
    for layer in range(depth):
        j = layer // 2
        if layer % 2 == 0:
            proj = _proj(x2d, ev_w_in[j].astype(BF16))
            proj3 = proj.reshape(bsz, l, proj.shape[1])
            ya = _hgrn2(proj3, lower_bounds[layer], ev_a_norm[j])
            y2 = _s5_core(proj, bsz, ev_s5_a_re[j], ev_s5_a_im[j], ev_s5_log_dt[j],
                          ev_s5_b_re[j], ev_s5_b_im[j], ev_s5_c_re[j], ev_s5_c_im[j])
            yb = _s5_glu(y2, proj, ev_s5_d[j], ev_s5_w_glu[j].astype(BF16))
            x2d = _mix_ln(ya.reshape(t, A_W), yb, x2d, ev_w_out[j].astype(BF16), ln1_g[layer], ln1_b[layer], alpha)
        else:
            proj = _proj(x2d, od_w_in[j].astype(BF16))
            proj3 = proj.reshape(bsz, l, proj.shape[1])
            yc = _swa(proj3, od_sinks[j])
            yd = _moba(proj3)
            x2d = _mix_ln(yc.reshape(t, -1), yd.reshape(t, -1), x2d, od_w_out[j].astype(BF16),
                          ln1_g[layer], ln1_b[layer], alpha)
        x2d = _moe_ln(x2d, moe_w_group[layer], moe_b_group[layer], moe_w_expert[layer], moe_b_expert[layer],
                      moe_w_gate_up[layer].astype(BF16), moe_w_down[layer].astype(BF16),
                      ln2_g[layer], ln2_b[layer], alpha)
    return x2d.reshape(bsz, l, d)
```

```python
import functools
import math

import jax
import jax.numpy as jnp
from jax import lax
from jax.experimental import pallas as pl
from jax.experimental.pallas import tpu as pltpu

F32 = jnp.float32
BF16 = jnp.bfloat16
NEG = -1e30
LN_EPS = 1e-5
RMS_EPS = 1e-6

A_HEADS = 4
A_DK = 128
A_DV = 128
A_CHUNK = 64
A_W = A_HEADS * A_DK

B_WIDTH = 512
B_GROUP = 16
B_GROUPS = B_WIDTH // B_GROUP
B_STATE = 64
S5_LC = 16
S5_CW = S5_LC * B_GROUP
S5_SLAB_LANES = 128

HEAD_DIM = 64
C_HEADS = 8
C_KV = 2
WINDOW = 128
D_HEADS = 8
D_KV = 2
MOBA_BLOCK = 256
MOBA_TOPK = 3
MOBA_TRIP = 4
LOG2E = math.log2(math.e)
BF16_SUBLANES = 16
GQA = C_HEADS // C_KV

N_GROUPS = 4
EXP_PER_GROUP = 4
N_EXPERTS = 16
EXP_HIDDEN = 256

VMEM_LIMIT_BYTES = 48 * 1024 * 1024


def _cparams(*sem):
    return pltpu.CompilerParams(dimension_semantics=sem, vmem_limit_bytes=VMEM_LIMIT_BYTES)


def _nt_dot(a, b):
    return lax.dot_general(a, b, (((1,), (1,)), ((), ())), preferred_element_type=F32)


def _dot(a, b):
    return jnp.dot(a, b, preferred_element_type=F32)


def _split_bf16(v):
    hi = v.astype(BF16)
    lo = (v - hi.astype(F32)).astype(BF16)
    return hi, lo


def _silu(v):
    return v * jax.nn.sigmoid(v)


def _layer_norm_rows(r, g, b):
    mu = jnp.mean(r, axis=-1, keepdims=True)
    d = r - mu
    var = jnp.mean(d * d, axis=-1, keepdims=True)
    return d * lax.rsqrt(var + LN_EPS) * g + b


def _proj_kernel(x_ref, w_ref, o_ref):
    o_ref[...] = _dot(x_ref[...].astype(BF16), w_ref[...])


def _proj(x2d, w_bf, tm=512):
    t, d = x2d.shape
    n = w_bf.shape[1]
    return pl.pallas_call(
        _proj_kernel,
        grid=(t // tm,),
        in_specs=[pl.BlockSpec((tm, d), lambda i: (i, 0)),
                  pl.BlockSpec((d, n), lambda i: (0, 0))],
        out_specs=pl.BlockSpec((tm, n), lambda i: (i, 0)),
        out_shape=jax.ShapeDtypeStruct((t, n), F32),
        compiler_params=_cparams("parallel"),
        name="in_proj",
    )(x2d, w_bf)


def _mix_ln_kernel(ya_ref, yb_ref, x_ref, w_ref, g_ref, b_ref, o_ref, *, alpha):
    half = ya_ref.shape[1]
    mix = _dot(ya_ref[...].astype(BF16), w_ref[0:half, :]) + _dot(yb_ref[...].astype(BF16), w_ref[half:, :])
    o_ref[...] = _layer_norm_rows(alpha * x_ref[...] + mix, g_ref[...], b_ref[...])


def _mix_ln(ya, yb, x2d, w_out_bf, g, b, alpha, tm=512):
    t, d = x2d.shape
    half = ya.shape[1]
    return pl.pallas_call(
        functools.partial(_mix_ln_kernel, alpha=alpha),
        grid=(t // tm,),
        in_specs=[pl.BlockSpec((tm, half), lambda i: (i, 0)),
                  pl.BlockSpec((tm, half), lambda i: (i, 0)),
                  pl.BlockSpec((tm, d), lambda i: (i, 0)),
                  pl.BlockSpec((2 * half, d), lambda i: (0, 0)),
                  pl.BlockSpec((1, d), lambda i: (0, 0)),
                  pl.BlockSpec((1, d), lambda i: (0, 0))],
        out_specs=pl.BlockSpec((tm, d), lambda i: (i, 0)),
        out_shape=jax.ShapeDtypeStruct((t, d), F32),
        compiler_params=_cparams("parallel"),
        name="mix_ln",
    )(ya, yb, x2d, w_out_bf, g.reshape(1, d), b.reshape(1, d))


def _hgrn2_kernel(proj_ref, lb_ref, ng_ref, tri_ref, o_ref, st_ref, *, tb):
    @pl.when(pl.program_id(1) == 0)
    def _():
        st_ref[...] = jnp.zeros_like(st_ref)

    ql = proj_ref[:, 0:A_W]
    fl = proj_ref[:, A_W:2 * A_W]
    iv = proj_ref[:, 2 * A_W:3 * A_W]
    gt = proj_ref[:, 3 * A_W:4 * A_W]
    lb = lb_ref[...]
    f = lb + (1.0 - lb) * jax.nn.sigmoid(fl)
    k = 1.0 - f
    lf = jnp.log(f)
    lf_hi, lf_lo = _split_bf16(lf)
    tri = tri_ref[...]
    b = _dot(tri, lf_hi) + _dot(tri, lf_lo)
    qd = _silu(ql) * jnp.exp(b)
    ki = k * jnp.exp(-b)
    og = ng_ref[...] * _silu(gt)

    row = lax.broadcasted_iota(jnp.int32, (A_CHUNK, A_CHUNK), 0)
    col = lax.broadcasted_iota(jnp.int32, (A_CHUNK, A_CHUNK), 1)
    causal = col <= row
    v_t = [iv[:, h * A_DV:(h + 1) * A_DV].T for h in range(A_HEADS)]

    for c in range(tb // A_CHUNK):
        r0 = c * A_CHUNK
        b_c = b[r0:r0 + A_CHUNK, :]
        b_last = b[r0 + A_CHUNK - 1:r0 + A_CHUNK, :]
        kt = k[r0:r0 + A_CHUNK, :] * jnp.exp(b_last - b_c)
        dec = jnp.exp(b_last)
        for h in range(A_HEADS):
            ls = slice(h * A_DK, (h + 1) * A_DK)
            qd_h = qd[r0:r0 + A_CHUNK, ls].astype(BF16)
            ki_h = ki[r0:r0 + A_CHUNK, ls].astype(BF16)
            v_h = iv[r0:r0 + A_CHUNK, ls]
            att = jnp.where(causal, _nt_dot(qd_h, ki_h), 0.0)
            s_t = st_ref[h]
            o = _dot(att.astype(BF16), v_h.astype(BF16)) + _nt_dot(qd_h, s_t.astype(BF16))
            st_ref[h] = s_t * dec[:, ls] + _dot(v_t[h][:, r0:r0 + A_CHUNK].astype(BF16), kt[:, ls].astype(BF16))
            o = o * lax.rsqrt(jnp.mean(o * o, axis=-1, keepdims=True) + RMS_EPS)
            o_ref[r0:r0 + A_CHUNK, ls] = o * og[r0:r0 + A_CHUNK, ls]


def _hgrn2(proj, lb, norm_g, tb=256):
    bsz, l, _ = proj.shape
    tb = min(tb, l)
    idx = jnp.arange(tb)
    tri = ((idx[:, None] >= idx[None, :]) & (idx[:, None] // A_CHUNK == idx[None, :] // A_CHUNK)).astype(BF16)
    return pl.pallas_call(
        functools.partial(_hgrn2_kernel, tb=tb),
        grid=(bsz, l // tb),
        in_specs=[pl.BlockSpec((None, tb, 4 * A_W), lambda b, t: (b, t, 0)),
                  pl.BlockSpec((1, A_W), lambda b, t: (0, 0)),
                  pl.BlockSpec((1, A_W), lambda b, t: (0, 0)),
                  pl.BlockSpec((tb, tb), lambda b, t: (0, 0))],
        out_specs=pl.BlockSpec((None, tb, A_W), lambda b, t: (b, t, 0)),
        out_shape=jax.ShapeDtypeStruct((bsz, l, A_W), F32),
        scratch_shapes=[pltpu.VMEM((A_HEADS, A_DV, A_DK), F32)],
        compiler_params=_cparams("parallel", "arbitrary"),
        name="hgrn2",
    )(proj, lb.reshape(1, A_W), norm_g.reshape(1, A_W), tri)


def _s5_weights(a_re, a_im, log_dt, b_re, b_im, c_re, c_im):
    g, p = a_re.shape
    lc = S5_LC
    dt = jnp.exp(log_dt.astype(F32))[:, None]
    ar, ai = a_re.astype(F32), a_im.astype(F32)
    mag = jnp.exp(dt * ar)
    abar_re, abar_im = mag * jnp.cos(dt * ai), mag * jnp.sin(dt * ai)
    den = ar * ar + ai * ai
    xr, xi = abar_re - 1.0, abar_im
    fr = (xr * ar + xi * ai) / den
    fi = (xi * ar - xr * ai) / den
    br, bi = b_re.astype(F32), b_im.astype(F32)
    bb_re = fr[..., None] * br - fi[..., None] * bi
    bb_im = fr[..., None] * bi + fi[..., None] * br
    cr, ci = c_re.astype(F32), c_im.astype(F32)
    tau = jnp.arange(lc + 1, dtype=F32)[:, None, None]
    pmag = jnp.exp(tau * (dt * ar)[None])
    pw_re = pmag * jnp.cos(tau * (dt * ai)[None])
    pw_im = pmag * jnp.sin(tau * (dt * ai)[None])
    hp = lax.Precision.HIGHEST
    ab_re = pw_re[..., None] * bb_re[None] - pw_im[..., None] * bb_im[None]
    ab_im = pw_re[..., None] * bb_im[None] + pw_im[..., None] * bb_re[None]
    kk = (jnp.einsum('gnp,tgpm->tgnm', cr, ab_re, precision=hp)
          - jnp.einsum('gnp,tgpm->tgnm', ci, ab_im, precision=hp))
    s_idx = jnp.arange(lc)[:, None]
    t_idx = jnp.arange(lc)[None, :]
    lag = t_idx - s_idx
    toe = jnp.where((lag >= 0)[:, :, None, None, None], kk[jnp.clip(lag, 0, lc)], 0.0)
    toe = toe.transpose(2, 0, 4, 1, 3).reshape(g, lc * B_GROUP, lc * B_GROUP)
    rev = lc - 1 - jnp.arange(lc)
    wx_re = ab_re[rev].transpose(1, 0, 3, 2).reshape(g, lc * B_GROUP, p)
    wx_im = ab_im[rev].transpose(1, 0, 3, 2).reshape(g, lc * B_GROUP, p)
    pr, pi_ = pw_re[1:], pw_im[1:]
    wy_re = cr[None] * pr[:, :, None, :] - ci[None] * pi_[:, :, None, :]
    wy_im = -(cr[None] * pi_[:, :, None, :] + ci[None] * pr[:, :, None, :])
    wy_re = wy_re.transpose(1, 3, 0, 2).reshape(g, p, lc * B_GROUP)
    wy_im = wy_im.transpose(1, 3, 0, 2).reshape(g, p, lc * B_GROUP)
    return toe, wx_re, wx_im, wy_re, wy_im, pw_re[lc], pw_im[lc]


def _pair_pack_rows(w):
    g, r, c = w.shape
    w = w.reshape(g // 2, 2, r, c)
    z = jnp.zeros((g // 2, r, c), w.dtype)
    top = jnp.concatenate([w[:, 0], z], axis=2)
    bot = jnp.concatenate([z, w[:, 1]], axis=2)
    return jnp.concatenate([top, bot], axis=1)


def _s5_inject_kernel(u_ref, wre_ref, wim_ref, xre_ref, xim_ref):
    lhs = jnp.concatenate([u_ref[0], u_ref[1]], axis=1)
    xre_ref[...] = _dot(lhs, wre_ref[...])
    xim_ref[...] = _dot(lhs, wim_ref[...])


def _s5_scan_kernel(xre_ref, xim_ref, are_ref, aim_ref, hre_ref, him_ref, sre_ref, sim_ref, *, tcs):
    @pl.when(pl.program_id(0) == 0)
    def _():
        sre_ref[...] = jnp.zeros_like(sre_ref)
        sim_ref[...] = jnp.zeros_like(sim_ref)

    ar = are_ref[...]
    ai = aim_ref[...]

    def body(r, carry):
        hr, hi = carry
        hre_ref[r] = hr
        him_ref[r] = hi
        return ar * hr - ai * hi + xre_ref[r], ar * hi + ai * hr + xim_ref[r]

    hr, hi = lax.fori_loop(0, tcs, body, (sre_ref[...], sim_ref[...]), unroll=8)
    sre_ref[...] = hr
    sim_ref[...] = hi


def _s5_readout_kernel(u_ref, toe_ref, wyre_ref, wyim_ref, hre_ref, him_ref, y_ref):
    hre = hre_ref[...].astype(BF16)
    him = him_ref[...].astype(BF16)
    for r in range(2):
        y_ref[r] = (_dot(u_ref[r], toe_ref[r])
                    + _dot(hre, wyre_ref[r]) + _dot(him, wyim_ref[r]))


def _s5_pack_kernel(*refs):
    u_refs, o_ref = refs[:-1], refs[-1]
    tc = o_ref.shape[1]
    gps = S5_SLAB_LANES // B_GROUP
    for r, u_ref in enumerate(u_refs):
        rows = [u_ref[pl.ds(s, tc, stride=S5_LC), :] for s in range(S5_LC)]
        for j in range(gps):
            piece = jnp.concatenate([row[:, j * B_GROUP:(j + 1) * B_GROUP] for row in rows], axis=1)
            o_ref[r * gps + j] = piece.astype(BF16)


def _s5_pack(proj2d, tc=128):
    t, wd = proj2d.shape
    nc = t // S5_LC
    tc = min(tc, nc)
    col0 = (wd - B_WIDTH) // S5_SLAB_LANES
    nslab = B_WIDTH // S5_SLAB_LANES
    return pl.pallas_call(
        _s5_pack_kernel,
        grid=(nc // tc,),
        in_specs=[pl.BlockSpec((tc * S5_LC, S5_SLAB_LANES), functools.partial(lambda i, r: (i, col0 + r), r=r))
                  for r in range(nslab)],
        out_specs=pl.BlockSpec((B_GROUPS, tc, S5_CW), lambda i: (0, i, 0)),
        out_shape=jax.ShapeDtypeStruct((B_GROUPS, nc, S5_CW), BF16),
        compiler_params=_cparams("parallel"),
        name="s5_pack",
    )(*([proj2d] * nslab))


def _s5_core(proj2d, bsz, a_re, a_im, log_dt, b_re, b_im, c_re, c_im):
    l = proj2d.shape[0] // bsz
    g, p, lc, cw = B_GROUPS, B_STATE, S5_LC, S5_CW
    ncb = l // lc
    toe, wx_re, wx_im, wy_re, wy_im, a_re_lc, a_im_lc = _s5_weights(a_re, a_im, log_dt, b_re, b_im, c_re, c_im)
    toe = toe.astype(BF16)
    wx_re_p = _pair_pack_rows(wx_re).astype(BF16)
    wx_im_p = _pair_pack_rows(wx_im).astype(BF16)
    zero = jnp.zeros_like(wy_re)
    even = (jnp.arange(g) % 2 == 0)[:, None, None]
    wy_re_p = jnp.where(even, jnp.concatenate([wy_re, zero], 1), jnp.concatenate([zero, wy_re], 1)).astype(BF16)
    wy_im_p = jnp.where(even, jnp.concatenate([wy_im, zero], 1), jnp.concatenate([zero, wy_im], 1)).astype(BF16)

    u2 = _s5_pack(proj2d)
    tc = min(1024, ncb)
    nct = ncb // tc
    slab = 2 * p
    x_shape = jax.ShapeDtypeStruct((ncb, bsz * g * p), F32)
    x_spec = pl.BlockSpec((tc, slab), lambda j, b, i: (i, b * (g // 2) + j))
    u_spec = pl.BlockSpec((2, tc, cw), lambda j, b, i: (j, b * nct + i, 0))
    xre, xim = pl.pallas_call(
        _s5_inject_kernel,
        grid=(g // 2, bsz, nct),
        in_specs=[u_spec,
                  pl.BlockSpec((None, 2 * cw, slab), lambda j, b, i: (j, 0, 0)),
                  pl.BlockSpec((None, 2 * cw, slab), lambda j, b, i: (j, 0, 0))],
        out_specs=[x_spec, x_spec],
        out_shape=[x_shape, x_shape],
        compiler_params=_cparams("parallel", "parallel", "parallel"),
        name="s5_inject",
    )(u2, wx_re_p, wx_im_p)

    lanes = 512
    rows = bsz * g * p // lanes
    tcs = min(128, ncb)
    a_tile = lambda a: jnp.tile(a.reshape(g * p // lanes, lanes), (bsz, 1))
    seq_spec = pl.BlockSpec((tcs, rows, lanes), lambda i: (i, 0, 0))
    par_spec = pl.BlockSpec((rows, lanes), lambda i: (0, 0))
    seq_shape = jax.ShapeDtypeStruct((ncb, rows, lanes), F32)
    hre, him = pl.pallas_call(
        functools.partial(_s5_scan_kernel, tcs=tcs),
        grid=(ncb // tcs,),
        in_specs=[seq_spec, seq_spec, par_spec, par_spec],
        out_specs=[seq_spec, seq_spec],
        out_shape=[seq_shape, seq_shape],
        scratch_shapes=[pltpu.VMEM((rows, lanes), F32), pltpu.VMEM((rows, lanes), F32)],
        compiler_params=_cparams("arbitrary"),
        name="s5_scan",
    )(xre.reshape(ncb, rows, lanes), xim.reshape(ncb, rows, lanes), a_tile(a_re_lc), a_tile(a_im_lc))
    hre = hre.reshape(ncb, bsz * g * p)
    him = him.reshape(ncb, bsz * g * p)

    y2 = pl.pallas_call(
        _s5_readout_kernel,
        grid=(g // 2, bsz, nct),
        in_specs=[u_spec,
                  pl.BlockSpec((2, cw, cw), lambda j, b, i: (j, 0, 0)),
                  pl.BlockSpec((2, slab, cw), lambda j, b, i: (j, 0, 0)),
                  pl.BlockSpec((2, slab, cw), lambda j, b, i: (j, 0, 0)),
                  x_spec, x_spec],
        out_specs=u_spec,
        out_shape=jax.ShapeDtypeStruct((g, bsz * ncb, cw), F32),
        compiler_params=_cparams("parallel", "parallel", "parallel"),
        name="s5_readout",
    )(u2, toe, wy_re_p, wy_im_p, hre, him)
    return y2


def _s5_glu_kernel(y2_ref, u_ref, d_ref, w_ref, o_ref, yc_ref):
    tc = y2_ref.shape[1]
    gps = S5_SLAB_LANES // B_GROUP
    for r in range(yc_ref.shape[0]):
        grp = [y2_ref[r * gps + j] for j in range(gps)]
        for s in range(S5_LC):
            yc_ref[r, pl.ds(s, tc, stride=S5_LC), :] = jnp.concatenate(
                [blk[:, s * B_GROUP:(s + 1) * B_GROUP] for blk in grp], axis=1)
    yc = jnp.concatenate([yc_ref[r] for r in range(yc_ref.shape[0])], axis=1)
    y = jax.nn.gelu(yc + d_ref[...] * u_ref[...])
    z = _dot(y.astype(BF16), w_ref[...])
    o_ref[...] = z[:, :B_WIDTH] * jax.nn.sigmoid(z[:, B_WIDTH:])


def _s5_glu(y2, proj2d, d_skip, w_glu_bf, tc=64):
    t = proj2d.shape[0]
    nc = t // S5_LC
    tc = min(tc, nc)
    tm = tc * S5_LC
    ucol = proj2d.shape[1] // B_WIDTH - 1
    return pl.pallas_call(
        _s5_glu_kernel,
        grid=(nc // tc,),
        in_specs=[pl.BlockSpec((B_GROUPS, tc, S5_CW), lambda i: (0, i, 0)),
                  pl.BlockSpec((tm, B_WIDTH), lambda i: (i, ucol)),
                  pl.BlockSpec((1, B_WIDTH), lambda i: (0, 0)),
                  pl.BlockSpec((B_WIDTH, 2 * B_WIDTH), lambda i: (0, 0))],
        out_specs=pl.BlockSpec((tm, B_WIDTH), lambda i: (i, 0)),
        out_shape=jax.ShapeDtypeStruct((t, B_WIDTH), F32),
        scratch_shapes=[pltpu.VMEM((B_WIDTH // S5_SLAB_LANES, tm, S5_SLAB_LANES), F32)],
        compiler_params=_cparams("parallel"),
        name="s5_glu",
    )(y2, proj2d, d_skip.reshape(1, B_WIDTH), w_glu_bf)


def _swa_kernel(sink_ref, q_ref, kp_ref, kc_ref, vp_ref, vc_ref, o_ref):
    first = pl.program_id(1) == 0
    w = WINDOW
    cols = GQA * w
    kpos = lax.broadcasted_iota(jnp.int32, (2 * w, cols), 0)
    qpos = lax.broadcasted_iota(jnp.int32, (2 * w, cols), 1) % w + w
    valid = (kpos <= qpos) & (qpos - kpos < w) & jnp.logical_not(first & (kpos < w))
    colhead = lax.broadcasted_iota(jnp.int32, (1, cols), 1) // w
    qt = q_ref[...].T * (HEAD_DIM ** -0.5 * LOG2E)
    vt = jnp.concatenate([vp_ref[...].T, vc_ref[...].T], axis=1)
    ones_rows = (lax.broadcasted_iota(jnp.int32, (BF16_SUBLANES, 2 * w), 0) == 0).astype(BF16)
    outs = []
    for kv in range(C_KV):
        ls = slice(kv * HEAD_DIM, (kv + 1) * HEAD_DIM)
        k2 = jnp.concatenate([kp_ref[:, ls], kc_ref[:, ls]], axis=0).astype(BF16)
        q_t = jnp.concatenate([qt[(kv * GQA + g) * HEAD_DIM:(kv * GQA + g + 1) * HEAD_DIM, :]
                               for g in range(GQA)], axis=1).astype(BF16)
        s = jnp.where(valid, _dot(k2, q_t), NEG)
        sink = jnp.zeros((1, cols), F32)
        for g in range(GQA):
            sink = jnp.where(colhead == g, sink_ref[kv * GQA + g] * LOG2E, sink)
        m = jnp.maximum(jnp.max(s, axis=0, keepdims=True), sink)
        p = jnp.exp2(s - m)
        vt_aug = jnp.concatenate([vt[ls, :].astype(BF16), ones_rows], axis=0)
        acc = _dot(vt_aug, p.astype(BF16))
        o_t = acc[0:HEAD_DIM, :] / (acc[HEAD_DIM:HEAD_DIM + 1, :] + jnp.exp2(sink - m))
        outs.extend(o_t[:, g * w:(g + 1) * w] for g in range(GQA))
    o_ref[...] = jnp.concatenate(outs, axis=0).T


def _swa(proj, sinks):
    bsz, l, _ = proj.shape
    w = WINDOW
    qw = C_HEADS * HEAD_DIM
    kvw = C_KV * HEAD_DIM
    kcol = qw // kvw
    prev = lambda b, n: (b, jnp.maximum(n - 1, 0), kcol)
    prev_v = lambda b, n: (b, jnp.maximum(n - 1, 0), kcol + 1)
    return pl.pallas_call(
        _swa_kernel,
        grid=(bsz, l // w),
        in_specs=[pl.BlockSpec(memory_space=pltpu.SMEM),
                  pl.BlockSpec((None, w, qw), lambda b, n: (b, n, 0)),
                  pl.BlockSpec((None, w, kvw), prev),
                  pl.BlockSpec((None, w, kvw), lambda b, n: (b, n, kcol)),
                  pl.BlockSpec((None, w, kvw), prev_v),
                  pl.BlockSpec((None, w, kvw), lambda b, n: (b, n, kcol + 1))],
        out_specs=pl.BlockSpec((None, w, qw), lambda b, n: (b, n, 0)),
        out_shape=jax.ShapeDtypeStruct((bsz, l, qw), F32),
        compiler_params=_cparams("parallel", "arbitrary"),
        name="swa",
    )(sinks.astype(F32), proj, proj, proj, proj, proj)


def _kmean_kernel(k_ref, o_ref):
    nb = o_ref.shape[0]
    o_ref[...] = jnp.mean(k_ref[...].reshape(nb, MOBA_BLOCK, k_ref.shape[-1]), axis=1)


def _moba_kernel(q_ref, k_ref, vt_ref, km_ref, o_ref, qaug_ref, acc_ref, sa_ref, sb_ref):
    kvh = pl.program_id(1)
    i = pl.program_id(2)
    blk = MOBA_BLOCK
    cols = GQA * blk
    nb = km_ref.shape[0]
    scale = HEAD_DIM ** -0.5
    qt = q_ref[...].T * scale
    q_t = jnp.concatenate([qt[g * HEAD_DIM:(g + 1) * HEAD_DIM, :] for g in range(GQA)], axis=1)
    q_hi, q_lo = _split_bf16(q_t)

    km = jnp.where(kvh == 0, km_ref[:, :HEAD_DIM], km_ref[:, HEAD_DIM:])
    km_hi, km_lo = _split_bf16(km)
    gate = _dot(km_hi, q_hi) + _dot(km_hi, q_lo) + _dot(km_lo, q_hi)
    blkid = lax.broadcasted_iota(jnp.int32, (nb, cols), 0).astype(F32)
    past = blkid < i.astype(F32)
    gate = jnp.where(past, gate, NEG)
    sel = jnp.zeros((nb, cols), F32)
    for _ in range(min(MOBA_TOPK, nb)):
        top = jnp.max(gate, axis=0, keepdims=True)
        first = jnp.min(jnp.where(gate == top, blkid, float(nb)), axis=0, keepdims=True)
        hit = blkid == first
        sel = jnp.where(hit & past, 1.0, sel)
        gate = jnp.where(hit, -jnp.inf, gate)
    q2 = (q_t * LOG2E).astype(BF16)
    qaug_ref[0:HEAD_DIM, :] = q2
    pad = qaug_ref.shape[0] - HEAD_DIM - nb
    mask = jnp.where(sel > 0.0, 0.0, NEG)
    if pad:
        mask = jnp.concatenate([mask, jnp.zeros((pad, cols), F32)], axis=0)
    qaug_ref[HEAD_DIM:, :] = mask.astype(BF16)

    def k_rows(j):
        return k_ref[pl.ds(pl.multiple_of(j * blk, blk), blk), :]

    def scores(j):
        return _dot(k_rows(j), qaug_ref[...])

    s = _dot(k_rows(i)[:, :HEAD_DIM], q2)
    kpos = lax.broadcasted_iota(jnp.int32, (blk, cols), 0)
    qpos = lax.broadcasted_iota(jnp.int32, (blk, cols), 1) % blk
    s = jnp.where(kpos <= qpos, s, NEG)
    m0 = jnp.max(s, axis=0, keepdims=True)
    p = jnp.exp2(s - m0)
    acc_ref[...] = _dot(vt_ref[i], p.astype(BF16))

    def stage(s_ref, j):
        s_j = scores(j)
        s_ref[...] = s_j
        return jnp.max(s_j, axis=0, keepdims=True)

    def absorb(s_ref, s_max, j, m_old):
        m_new = jnp.maximum(m_old, s_max)
        p_j = jnp.exp2(s_ref[...] - m_new)
        corr = jnp.exp2(m_old - m_new)
        acc_ref[...] = acc_ref[...] * corr + _dot(vt_ref[j], p_j.astype(BF16))
        return m_new

    def body(t, carry):
        m, max_cur = carry
        j0 = MOBA_TRIP * t
        bufs = (sa_ref, sb_ref)
        for u in range(MOBA_TRIP):
            max_next = stage(bufs[(u + 1) % 2], jnp.minimum(j0 + u + 1, nb - 1))
            m = absorb(bufs[u % 2], max_cur, j0 + u, m)
            max_cur = max_next
        return m, max_cur

    lax.fori_loop(0, (i + MOBA_TRIP - 1) // MOBA_TRIP, body, (m0, stage(sa_ref, 0)))
    out_t = acc_ref[0:HEAD_DIM, :] / acc_ref[HEAD_DIM:HEAD_DIM + 1, :]
    stacked = jnp.concatenate([out_t[:, g * blk:(g + 1) * blk] for g in range(GQA)], axis=0)
    o_ref[...] = stacked.T


def _moba(proj):
    bsz, l, _ = proj.shape
    blk = MOBA_BLOCK
    nb = l // blk
    kvw = D_KV * HEAD_DIM
    qgw = GQA * HEAD_DIM
    q0 = C_HEADS * HEAD_DIM + 2 * C_KV * HEAD_DIM
    k0 = q0 + D_HEADS * HEAD_DIM
    qcol0 = q0 // qgw
    kmean = pl.pallas_call(
        _kmean_kernel,
        grid=(bsz,),
        in_specs=[pl.BlockSpec((None, l, kvw), lambda b: (b, 0, k0 // kvw))],
        out_specs=pl.BlockSpec((None, nb, kvw), lambda b: (b, 0, 0)),
        out_shape=jax.ShapeDtypeStruct((bsz, nb, kvw), F32),
        compiler_params=_cparams("parallel"),
        name="moba_kmean",
    )(proj)
    nbp = -(-nb // BF16_SUBLANES) * BF16_SUBLANES
    k_bf = proj[:, :, k0:k0 + kvw].reshape(bsz, l, D_KV, HEAD_DIM).transpose(0, 2, 1, 3).astype(BF16)
    onehot = (jnp.arange(l)[:, None] // blk == jnp.arange(nbp)[None, :]).astype(BF16)
    k_aug = jnp.concatenate([k_bf, jnp.broadcast_to(onehot, (bsz, D_KV, l, nbp))], axis=-1)
    vt_bf = (proj[:, :, k0 + kvw:k0 + 2 * kvw].reshape(bsz, nb, blk, D_KV, HEAD_DIM)
             .transpose(0, 3, 1, 4, 2).astype(BF16))
    ones_rows = jnp.zeros((bsz, D_KV, nb, BF16_SUBLANES, blk), BF16).at[:, :, :, 0, :].set(1.0)
    vt_bf = jnp.concatenate([vt_bf, ones_rows], axis=3)
    vh = HEAD_DIM + BF16_SUBLANES
    cols = GQA * blk
    kw = HEAD_DIM + nbp
    return pl.pallas_call(
        _moba_kernel,
        grid=(bsz, D_KV, nb),
        in_specs=[pl.BlockSpec((None, blk, qgw), lambda b, h, i: (b, i, qcol0 + h)),
                  pl.BlockSpec((None, None, l, kw), lambda b, h, i: (b, h, 0, 0)),
                  pl.BlockSpec((None, None, nb, vh, blk), lambda b, h, i: (b, h, 0, 0, 0)),
                  pl.BlockSpec((None, nb, kvw), lambda b, h, i: (b, 0, 0))],
        out_specs=pl.BlockSpec((None, blk, qgw), lambda b, h, i: (b, i, h)),
        out_shape=jax.ShapeDtypeStruct((bsz, l, D_HEADS * HEAD_DIM), F32),
        scratch_shapes=[pltpu.VMEM((kw, cols), BF16), pltpu.VMEM((vh, cols), F32),
                        pltpu.VMEM((blk, cols), F32), pltpu.VMEM((blk, cols), F32)],
        compiler_params=_cparams("parallel", "arbitrary", "arbitrary"),
        name="moba",
    )(proj, k_aug, vt_bf, kmean)


ROUTER_LANES = 128


def _router_logits(x, wr_ref, br_ref):
    x_hi, x_lo = _split_bf16(x)
    w_hi, w_lo = _split_bf16(wr_ref[...])
    return _dot(x_hi, w_hi) + _dot(x_lo, w_hi) + _dot(x_hi, w_lo) + br_ref[...]


def _lane_ids(shape):
    return lax.broadcasted_iota(jnp.int32, shape, 1).astype(F32)


def _top_group(logits):
    lane = _lane_ids(logits.shape)
    gl = jnp.where(lane < N_GROUPS, logits, -jnp.inf)
    gmax = jnp.max(gl, axis=-1, keepdims=True)
    return jnp.min(jnp.where(gl == gmax, lane, float(ROUTER_LANES)), axis=-1, keepdims=True)


def _moe_gates(logits, g_idx):
    lane = _lane_ids(logits.shape)
    big = float(ROUTER_LANES)
    gl = jnp.where(lane < N_GROUPS, logits, -jnp.inf)
    gmax = jnp.max(gl, axis=-1, keepdims=True)
    g_logit = jnp.sum(jnp.where(lane == g_idx, logits, 0.0), axis=-1, keepdims=True)
    g_top = jnp.exp(g_logit - gmax) / jnp.sum(jnp.exp(gl - gmax), axis=-1, keepdims=True)
    lo = N_GROUPS + EXP_PER_GROUP * g_idx
    el = jnp.where((lane >= lo) & (lane < lo + EXP_PER_GROUP), logits, -jnp.inf)
    m1 = jnp.max(el, axis=-1, keepdims=True)
    i1 = jnp.min(jnp.where(el == m1, lane, big), axis=-1, keepdims=True)
    el2 = jnp.where(lane == i1, -jnp.inf, el)
    m2 = jnp.max(el2, axis=-1, keepdims=True)
    i2 = jnp.min(jnp.where(el2 == m2, lane, big), axis=-1, keepdims=True)
    e21 = jnp.exp(m2 - m1)
    w1 = g_top / (1.0 + e21)
    w2 = w1 * e21
    return jnp.where(lane == i1, w1, 0.0) + jnp.where(lane == i2, w2, 0.0)


def _moe_kernel(x_ref, wr_ref, br_ref, wgu_ref, wd_ref, g_ref, b_ref, o_ref, gates_ref, xb_ref, acc_ref, *, alpha):
    e = pl.program_id(1)

    @pl.when(e == 0)
    def _():
        x = x_ref[...]
        logits = _router_logits(x, wr_ref, br_ref)
        gates_ref[...] = _moe_gates(logits, _top_group(logits))
        xb_ref[...] = x.astype(BF16)
        acc_ref[...] = jnp.zeros_like(acc_ref)

    gates = gates_ref[...]
    lane = lax.broadcasted_iota(jnp.int32, gates.shape, 1)
    gate_e = jnp.sum(jnp.where(lane == N_GROUPS + e, gates, 0.0), axis=-1, keepdims=True)
    hu = _dot(xb_ref[...], wgu_ref[...].astype(BF16))
    h = _silu(hu[:, :EXP_HIDDEN]) * hu[:, EXP_HIDDEN:] * gate_e
    acc_ref[...] += _dot(h.astype(BF16), wd_ref[...].astype(BF16))

    @pl.when(e == N_EXPERTS - 1)
    def _():
        o_ref[...] = _layer_norm_rows(alpha * x_ref[...] + acc_ref[...], g_ref[...], b_ref[...])


def _moe_ln(x2d, w_group, b_group, w_expert, b_expert, w_gate_up, w_down, g, b, alpha, layer=0, tm=1024):
    t, d = x2d.shape
    w_gate_up = w_gate_up.reshape((-1,) + w_gate_up.shape[-3:])
    w_down = w_down.reshape((-1,) + w_down.shape[-3:])
    tm = min(tm, t)
    pad = ROUTER_LANES - N_GROUPS - N_EXPERTS
    w_router = jnp.concatenate([w_group, w_expert, jnp.zeros((d, pad), F32)], axis=1).astype(F32)
    b_router = jnp.concatenate([b_group, b_expert, jnp.zeros((pad,), F32)]).astype(F32).reshape(1, ROUTER_LANES)
    return pl.pallas_call(
        functools.partial(_moe_kernel, alpha=alpha),
        grid=(t // tm, N_EXPERTS),
        in_specs=[pl.BlockSpec((tm, d), lambda i, e: (i, 0)),
                  pl.BlockSpec((d, ROUTER_LANES), lambda i, e: (0, 0)),
                  pl.BlockSpec((1, ROUTER_LANES), lambda i, e: (0, 0)),
                  pl.BlockSpec((None, None, d, 2 * EXP_HIDDEN), lambda i, e: (layer, e, 0, 0)),
                  pl.BlockSpec((None, None, EXP_HIDDEN, d), lambda i, e: (layer, e, 0, 0)),
                  pl.BlockSpec((1, d), lambda i, e: (0, 0)),
                  pl.BlockSpec((1, d), lambda i, e: (0, 0))],
        out_specs=pl.BlockSpec((tm, d), lambda i, e: (i, 0)),
        out_shape=jax.ShapeDtypeStruct((t, d), F32),
        scratch_shapes=[pltpu.VMEM((tm, ROUTER_LANES), F32), pltpu.VMEM((tm, d), BF16),
                        pltpu.VMEM((tm, d), F32)],
        compiler_params=_cparams("parallel", "arbitrary"),
        name="moe_ln",
    )(x2d, w_router, b_router, w_gate_up, w_down, g.reshape(1, d), b.reshape(1, d))


def kernel(x, hgrn_lb_logits, ev_w_in, ev_a_norm, ev_s5_a_re, ev_s5_a_im, ev_s5_log_dt, ev_s5_b_re, ev_s5_b_im, ev_s5_c_re, ev_s5_c_im, ev_s5_d, ev_s5_w_glu, ev_w_out, od_w_in, od_sinks, od_w_out, ln1_g, ln1_b, moe_w_group, moe_b_group, moe_w_expert, moe_b_expert, moe_w_gate_up, moe_w_down, ln2_g, ln2_b):
    bsz, l, d = x.shape
    depth = ln1_g.shape[0]
    alpha = (2.0 * depth) ** 0.25
    t = bsz * l
    lower_bounds = jnp.cumsum(jax.nn.softmax(hgrn_lb_logits.astype(F32), axis=0), axis=0)
    x2d = x.reshape(t, d)
    for layer in range(depth):
        j = layer // 2
        if layer % 2 == 0:
            proj = _proj(x2d, ev_w_in[j].astype(BF16))
            proj3 = proj.reshape(bsz, l, proj.shape[1])
            ya = _hgrn2(proj3, lower_bounds[layer], ev_a_norm[j])
            y2 = _s5_core(proj, bsz, ev_s5_a_re[j], ev_s5_a_im[j], ev_s5_log_dt[j],
                          ev_s5_b_re[j], ev_s5_b_im[j], ev_s5_c_re[j], ev_s5_c_im[j])
            yb = _s5_glu(y2, proj, ev_s5_d[j], ev_s5_w_glu[j].astype(BF16))
            x2d = _mix_ln(ya.reshape(t, A_W), yb, x2d, ev_w_out[j].astype(BF16), ln1_g[layer], ln1_b[layer], alpha)
        else:
            proj = _proj(x2d, od_w_in[j].astype(BF16))
            proj3 = proj.reshape(bsz, l, proj.shape[1])
            yc = _swa(proj3, od_sinks[j])
            yd = _moba(proj3)
            x2d = _mix_ln(yc.reshape(t, -1), yd.reshape(t, -1), x2d, od_w_out[j].astype(BF16),
                          ln1_g[layer], ln1_b[layer], alpha)
        x2d = _moe_ln(x2d, moe_w_group[layer], moe_b_group[layer], moe_w_expert[layer], moe_b_expert[layer],
                      moe_w_gate_up, moe_w_down, ln2_g[layer], ln2_b[layer], alpha, layer=layer)
    return x2d.reshape(bsz, l, d)
```

```python
import functools
import math

import jax
import jax.numpy as jnp
from jax import lax
from jax.experimental import pallas as pl
from jax.experimental.pallas import tpu as pltpu

F32 = jnp.float32
BF16 = jnp.bfloat16
NEG = -1e30
LN_EPS = 1e-5
RMS_EPS = 1e-6

A_HEADS = 4
A_DK = 128
A_DV = 128
A_CHUNK = 64
A_W = A_HEADS * A_DK

B_WIDTH = 512
B_GROUP = 16
B_GROUPS = B_WIDTH // B_GROUP
B_STATE = 64
S5_LC = 16
S5_CW = S5_LC * B_GROUP
S5_SLAB_LANES = 128

HEAD_DIM = 64
C_HEADS = 8
C_KV = 2
WINDOW = 128
D_HEADS = 8
D_KV = 2
MOBA_BLOCK = 256
MOBA_TOPK = 3
MOBA_TRIP = 4
LOG2E = math.log2(math.e)
BF16_SUBLANES = 16
GQA = C_HEADS // C_KV

N_GROUPS = 4
EXP_PER_GROUP = 4
N_EXPERTS = 16
EXP_HIDDEN = 256

VMEM_LIMIT_BYTES = 48 * 1024 * 1024


def _cparams(*sem):
    return pltpu.CompilerParams(dimension_semantics=sem, vmem_limit_bytes=VMEM_LIMIT_BYTES)


def _nt_dot(a, b):
    return lax.dot_general(a, b, (((1,), (1,)), ((), ())), preferred_element_type=F32)


def _dot(a, b):
    return jnp.dot(a, b, preferred_element_type=F32)


def _split_bf16(v):
    hi = v.astype(BF16)
    lo = (v - hi.astype(F32)).astype(BF16)
    return hi, lo


def _silu(v):
    return v * jax.nn.sigmoid(v)


def _layer_norm_rows(r, g, b):
    mu = jnp.mean(r, axis=-1, keepdims=True)
    d = r - mu
    var = jnp.mean(d * d, axis=-1, keepdims=True)
    return d * lax.rsqrt(var + LN_EPS) * g + b


def _proj_kernel(x_ref, w_ref, o_ref):
    o_ref[...] = _dot(x_ref[...].astype(BF16), w_ref[...])


def _proj(x2d, w_bf, tm=512):
    t, d = x2d.shape
    n = w_bf.shape[1]
    return pl.pallas_call(
        _proj_kernel,
        grid=(t // tm,),
        in_specs=[pl.BlockSpec((tm, d), lambda i: (i, 0)),
                  pl.BlockSpec((d, n), lambda i: (0, 0))],
        out_specs=pl.BlockSpec((tm, n), lambda i: (i, 0)),
        out_shape=jax.ShapeDtypeStruct((t, n), F32),
        compiler_params=_cparams("parallel"),
        name="in_proj",
    )(x2d, w_bf)


def _mix_ln_kernel(ya_ref, yb_ref, x_ref, w_ref, g_ref, b_ref, o_ref, *, alpha):
    half = ya_ref.shape[1]
    mix = _dot(ya_ref[...].astype(BF16), w_ref[0:half, :]) + _dot(yb_ref[...].astype(BF16), w_ref[half:, :])
    o_ref[...] = _layer_norm_rows(alpha * x_ref[...] + mix, g_ref[...], b_ref[...])


def _mix_ln(ya, yb, x2d, w_out_bf, g, b, alpha, tm=512):
    t, d = x2d.shape
    half = ya.shape[1]
    return pl.pallas_call(
        functools.partial(_mix_ln_kernel, alpha=alpha),
        grid=(t // tm,),
        in_specs=[pl.BlockSpec((tm, half), lambda i: (i, 0)),
                  pl.BlockSpec((tm, half), lambda i: (i, 0)),
                  pl.BlockSpec((tm, d), lambda i: (i, 0)),
                  pl.BlockSpec((2 * half, d), lambda i: (0, 0)),
                  pl.BlockSpec((1, d), lambda i: (0, 0)),
                  pl.BlockSpec((1, d), lambda i: (0, 0))],
        out_specs=pl.BlockSpec((tm, d), lambda i: (i, 0)),
        out_shape=jax.ShapeDtypeStruct((t, d), F32),
        compiler_params=_cparams("parallel"),
        name="mix_ln",
    )(ya, yb, x2d, w_out_bf, g.reshape(1, d), b.reshape(1, d))


def _hgrn2_kernel(proj_ref, lb_ref, ng_ref, tri_ref, o_ref, st_ref, *, tb):
    @pl.when(pl.program_id(1) == 0)
    def _():
        st_ref[...] = jnp.zeros_like(st_ref)

    ql = proj_ref[:, 0:A_W]
    fl = proj_ref[:, A_W:2 * A_W]
    iv = proj_ref[:, 2 * A_W:3 * A_W]
    gt = proj_ref[:, 3 * A_W:4 * A_W]
    lb = lb_ref[...]
    f = lb + (1.0 - lb) * jax.nn.sigmoid(fl)
    k = 1.0 - f
    lf = jnp.log(f)
    lf_hi, lf_lo = _split_bf16(lf)
    tri = tri_ref[...]
    b = _dot(tri, lf_hi) + _dot(tri, lf_lo)
    qd = _silu(ql) * jnp.exp(b)
    ki = k * jnp.exp(-b)
    og = ng_ref[...] * _silu(gt)

    row = lax.broadcasted_iota(jnp.int32, (A_CHUNK, A_CHUNK), 0)
    col = lax.broadcasted_iota(jnp.int32, (A_CHUNK, A_CHUNK), 1)
    causal = col <= row
    v_t = [iv[:, h * A_DV:(h + 1) * A_DV].T for h in range(A_HEADS)]

    for c in range(tb // A_CHUNK):
        r0 = c * A_CHUNK
        b_c = b[r0:r0 + A_CHUNK, :]
        b_last = b[r0 + A_CHUNK - 1:r0 + A_CHUNK, :]
        kt = k[r0:r0 + A_CHUNK, :] * jnp.exp(b_last - b_c)
        dec = jnp.exp(b_last)
        for h in range(A_HEADS):
            ls = slice(h * A_DK, (h + 1) * A_DK)
            qd_h = qd[r0:r0 + A_CHUNK, ls].astype(BF16)
            ki_h = ki[r0:r0 + A_CHUNK, ls].astype(BF16)
            v_h = iv[r0:r0 + A_CHUNK, ls]
            att = jnp.where(causal, _nt_dot(qd_h, ki_h), 0.0)
            s_t = st_ref[h]
            o = _dot(att.astype(BF16), v_h.astype(BF16)) + _nt_dot(qd_h, s_t.astype(BF16))
            st_ref[h] = s_t * dec[:, ls] + _dot(v_t[h][:, r0:r0 + A_CHUNK].astype(BF16), kt[:, ls].astype(BF16))
            o = o * lax.rsqrt(jnp.mean(o * o, axis=-1, keepdims=True) + RMS_EPS)
            o_ref[r0:r0 + A_CHUNK, ls] = o * og[r0:r0 + A_CHUNK, ls]


def _hgrn2(proj, lb, norm_g, tb=256):
    bsz, l, _ = proj.shape
    tb = min(tb, l)
    idx = jnp.arange(tb)
    tri = ((idx[:, None] >= idx[None, :]) & (idx[:, None] // A_CHUNK == idx[None, :] // A_CHUNK)).astype(BF16)
    return pl.pallas_call(
        functools.partial(_hgrn2_kernel, tb=tb),
        grid=(bsz, l // tb),
        in_specs=[pl.BlockSpec((None, tb, 4 * A_W), lambda b, t: (b, t, 0)),
                  pl.BlockSpec((1, A_W), lambda b, t: (0, 0)),
                  pl.BlockSpec((1, A_W), lambda b, t: (0, 0)),
                  pl.BlockSpec((tb, tb), lambda b, t: (0, 0))],
        out_specs=pl.BlockSpec((None, tb, A_W), lambda b, t: (b, t, 0)),
        out_shape=jax.ShapeDtypeStruct((bsz, l, A_W), F32),
        scratch_shapes=[pltpu.VMEM((A_HEADS, A_DV, A_DK), F32)],
        compiler_params=_cparams("parallel", "arbitrary"),
        name="hgrn2",
    )(proj, lb.reshape(1, A_W), norm_g.reshape(1, A_W), tri)


def _s5_weights(a_re, a_im, log_dt, b_re, b_im, c_re, c_im):
    g, p = a_re.shape
    lc = S5_LC
    dt = jnp.exp(log_dt.astype(F32))[:, None]
    ar, ai = a_re.astype(F32), a_im.astype(F32)
    mag = jnp.exp(dt * ar)
    abar_re, abar_im = mag * jnp.cos(dt * ai), mag * jnp.sin(dt * ai)
    den = ar * ar + ai * ai
    xr, xi = abar_re - 1.0, abar_im
    fr = (xr * ar + xi * ai) / den
    fi = (xi * ar - xr * ai) / den
    br, bi = b_re.astype(F32), b_im.astype(F32)
    bb_re = fr[..., None] * br - fi[..., None] * bi
    bb_im = fr[..., None] * bi + fi[..., None] * br
    cr, ci = c_re.astype(F32), c_im.astype(F32)
    tau = jnp.arange(lc + 1, dtype=F32)[:, None, None]
    pmag = jnp.exp(tau * (dt * ar)[None])
    pw_re = pmag * jnp.cos(tau * (dt * ai)[None])
    pw_im = pmag * jnp.sin(tau * (dt * ai)[None])
    hp = lax.Precision.HIGHEST
    ab_re = pw_re[..., None] * bb_re[None] - pw_im[..., None] * bb_im[None]
    ab_im = pw_re[..., None] * bb_im[None] + pw_im[..., None] * bb_re[None]
    kk = (jnp.einsum('gnp,tgpm->tgnm', cr, ab_re, precision=hp)
          - jnp.einsum('gnp,tgpm->tgnm', ci, ab_im, precision=hp))
    s_idx = jnp.arange(lc)[:, None]
    t_idx = jnp.arange(lc)[None, :]
    lag = t_idx - s_idx
    toe = jnp.where((lag >= 0)[:, :, None, None, None], kk[jnp.clip(lag, 0, lc)], 0.0)
    toe = toe.transpose(2, 0, 4, 1, 3).reshape(g, lc * B_GROUP, lc * B_GROUP)
    rev = lc - 1 - jnp.arange(lc)
    wx_re = ab_re[rev].transpose(1, 0, 3, 2).reshape(g, lc * B_GROUP, p)
    wx_im = ab_im[rev].transpose(1, 0, 3, 2).reshape(g, lc * B_GROUP, p)
    pr, pi_ = pw_re[1:], pw_im[1:]
    wy_re = cr[None] * pr[:, :, None, :] - ci[None] * pi_[:, :, None, :]
    wy_im = -(cr[None] * pi_[:, :, None, :] + ci[None] * pr[:, :, None, :])
    wy_re = wy_re.transpose(1, 3, 0, 2).reshape(g, p, lc * B_GROUP)
    wy_im = wy_im.transpose(1, 3, 0, 2).reshape(g, p, lc * B_GROUP)
    return toe, wx_re, wx_im, wy_re, wy_im, pw_re[lc], pw_im[lc]


def _pair_pack_rows(w):
    g, r, c = w.shape
    w = w.reshape(g // 2, 2, r, c)
    z = jnp.zeros((g // 2, r, c), w.dtype)
    top = jnp.concatenate([w[:, 0], z], axis=2)
    bot = jnp.concatenate([z, w[:, 1]], axis=2)
    return jnp.concatenate([top, bot], axis=1)


def _s5_inject_kernel(u_ref, wre_ref, wim_ref, xre_ref, xim_ref):
    lhs = jnp.concatenate([u_ref[0], u_ref[1]], axis=1)
    xre_ref[...] = _dot(lhs, wre_ref[...])
    xim_ref[...] = _dot(lhs, wim_ref[...])


def _s5_scan_kernel(xre_ref, xim_ref, are_ref, aim_ref, hre_ref, him_ref, sre_ref, sim_ref, *, tcs):
    @pl.when(pl.program_id(0) == 0)
    def _():
        sre_ref[...] = jnp.zeros_like(sre_ref)
        sim_ref[...] = jnp.zeros_like(sim_ref)

    ar = are_ref[...]
    ai = aim_ref[...]

    def body(r, carry):
        hr, hi = carry
        hre_ref[r] = hr
        him_ref[r] = hi
        return ar * hr - ai * hi + xre_ref[r], ar * hi + ai * hr + xim_ref[r]

    hr, hi = lax.fori_loop(0, tcs, body, (sre_ref[...], sim_ref[...]), unroll=8)
    sre_ref[...] = hr
    sim_ref[...] = hi


def _s5_readout_kernel(u_ref, toe_ref, wyre_ref, wyim_ref, hre_ref, him_ref, y_ref):
    hre = hre_ref[...].astype(BF16)
    him = him_ref[...].astype(BF16)
    for r in range(2):
        y_ref[r] = (_dot(u_ref[r], toe_ref[r])
                    + _dot(hre, wyre_ref[r]) + _dot(him, wyim_ref[r]))


def _s5_pack_kernel(*refs):
    u_refs, o_ref = refs[:-1], refs[-1]
    tc = o_ref.shape[1]
    gps = S5_SLAB_LANES // B_GROUP
    for r, u_ref in enumerate(u_refs):
        rows = [u_ref[pl.ds(s, tc, stride=S5_LC), :] for s in range(S5_LC)]
        for j in range(gps):
            piece = jnp.concatenate([row[:, j * B_GROUP:(j + 1) * B_GROUP] for row in rows], axis=1)
            o_ref[r * gps + j] = piece.astype(BF16)


def _s5_pack(proj2d, tc=128):
    t, wd = proj2d.shape
    nc = t // S5_LC
    tc = min(tc, nc)
    col0 = (wd - B_WIDTH) // S5_SLAB_LANES
    nslab = B_WIDTH // S5_SLAB_LANES
    return pl.pallas_call(
        _s5_pack_kernel,
        grid=(nc // tc,),
        in_specs=[pl.BlockSpec((tc * S5_LC, S5_SLAB_LANES), functools.partial(lambda i, r: (i, col0 + r), r=r))
                  for r in range(nslab)],
        out_specs=pl.BlockSpec((B_GROUPS, tc, S5_CW), lambda i: (0, i, 0)),
        out_shape=jax.ShapeDtypeStruct((B_GROUPS, nc, S5_CW), BF16),
        compiler_params=_cparams("parallel"),
        name="s5_pack",
    )(*([proj2d] * nslab))


def _s5_core(proj2d, bsz, a_re, a_im, log_dt, b_re, b_im, c_re, c_im):
    l = proj2d.shape[0] // bsz
    g, p, lc, cw = B_GROUPS, B_STATE, S5_LC, S5_CW
    ncb = l // lc
    toe, wx_re, wx_im, wy_re, wy_im, a_re_lc, a_im_lc = _s5_weights(a_re, a_im, log_dt, b_re, b_im, c_re, c_im)
    toe = toe.astype(BF16)
    wx_re_p = _pair_pack_rows(wx_re).astype(BF16)
    wx_im_p = _pair_pack_rows(wx_im).astype(BF16)
    zero = jnp.zeros_like(wy_re)
    even = (jnp.arange(g) % 2 == 0)[:, None, None]
    wy_re_p = jnp.where(even, jnp.concatenate([wy_re, zero], 1), jnp.concatenate([zero, wy_re], 1)).astype(BF16)
    wy_im_p = jnp.where(even, jnp.concatenate([wy_im, zero], 1), jnp.concatenate([zero, wy_im], 1)).astype(BF16)

    u2 = _s5_pack(proj2d)
    tc = min(1024, ncb)
    nct = ncb // tc
    slab = 2 * p
    x_shape = jax.ShapeDtypeStruct((ncb, bsz * g * p), F32)
    x_spec = pl.BlockSpec((tc, slab), lambda j, b, i: (i, b * (g // 2) + j))
    u_spec = pl.BlockSpec((2, tc, cw), lambda j, b, i: (j, b * nct + i, 0))
    xre, xim = pl.pallas_call(
        _s5_inject_kernel,
        grid=(g // 2, bsz, nct),
        in_specs=[u_spec,
                  pl.BlockSpec((None, 2 * cw, slab), lambda j, b, i: (j, 0, 0)),
                  pl.BlockSpec((None, 2 * cw, slab), lambda j, b, i: (j, 0, 0))],
        out_specs=[x_spec, x_spec],
        out_shape=[x_shape, x_shape],
        compiler_params=_cparams("parallel", "parallel", "parallel"),
        name="s5_inject",
    )(u2, wx_re_p, wx_im_p)

    lanes = 512
    rows = bsz * g * p // lanes
    tcs = min(128, ncb)
    a_tile = lambda a: jnp.tile(a.reshape(g * p // lanes, lanes), (bsz, 1))
    seq_spec = pl.BlockSpec((tcs, rows, lanes), lambda i: (i, 0, 0))
    par_spec = pl.BlockSpec((rows, lanes), lambda i: (0, 0))
    seq_shape = jax.ShapeDtypeStruct((ncb, rows, lanes), F32)
    hre, him = pl.pallas_call(
        functools.partial(_s5_scan_kernel, tcs=tcs),
        grid=(ncb // tcs,),
        in_specs=[seq_spec, seq_spec, par_spec, par_spec],
        out_specs=[seq_spec, seq_spec],
        out_shape=[seq_shape, seq_shape],
        scratch_shapes=[pltpu.VMEM((rows, lanes), F32), pltpu.VMEM((rows, lanes), F32)],
        compiler_params=_cparams("arbitrary"),
        name="s5_scan",
    )(xre.reshape(ncb, rows, lanes), xim.reshape(ncb, rows, lanes), a_tile(a_re_lc), a_tile(a_im_lc))
    hre = hre.reshape(ncb, bsz * g * p)
    him = him.reshape(ncb, bsz * g * p)

    y2 = pl.pallas_call(
        _s5_readout_kernel,
        grid=(g // 2, bsz, nct),
        in_specs=[u_spec,
                  pl.BlockSpec((2, cw, cw), lambda j, b, i: (j, 0, 0)),
                  pl.BlockSpec((2, slab, cw), lambda j, b, i: (j, 0, 0)),
                  pl.BlockSpec((2, slab, cw), lambda j, b, i: (j, 0, 0)),
                  x_spec, x_spec],
        out_specs=u_spec,
        out_shape=jax.ShapeDtypeStruct((g, bsz * ncb, cw), F32),
        compiler_params=_cparams("parallel", "parallel", "parallel"),
        name="s5_readout",
    )(u2, toe, wy_re_p, wy_im_p, hre, him)
    return y2


def _s5_glu_kernel(y2_ref, u_ref, d_ref, w_ref, o_ref, yc_ref):
    tc = y2_ref.shape[1]
    gps = S5_SLAB_LANES // B_GROUP
    for r in range(yc_ref.shape[0]):
        grp = [y2_ref[r * gps + j] for j in range(gps)]
        for s in range(S5_LC):
            yc_ref[r, pl.ds(s, tc, stride=S5_LC), :] = jnp.concatenate(
                [blk[:, s * B_GROUP:(s + 1) * B_GROUP] for blk in grp], axis=1)
    yc = jnp.concatenate([yc_ref[r] for r in range(yc_ref.shape[0])], axis=1)
    y = jax.nn.gelu(yc + d_ref[...] * u_ref[...])
    z = _dot(y.astype(BF16), w_ref[...])
    o_ref[...] = z[:, :B_WIDTH] * jax.nn.sigmoid(z[:, B_WIDTH:])


def _s5_glu(y2, proj2d, d_skip, w_glu_bf, tc=64):
    t = proj2d.shape[0]
    nc = t // S5_LC
    tc = min(tc, nc)
    tm = tc * S5_LC
    ucol = proj2d.shape[1] // B_WIDTH - 1
    return pl.pallas_call(
        _s5_glu_kernel,
        grid=(nc // tc,),
        in_specs=[pl.BlockSpec((B_GROUPS, tc, S5_CW), lambda i: (0, i, 0)),
                  pl.BlockSpec((tm, B_WIDTH), lambda i: (i, ucol)),
                  pl.BlockSpec((1, B_WIDTH), lambda i: (0, 0)),
                  pl.BlockSpec((B_WIDTH, 2 * B_WIDTH), lambda i: (0, 0))],
        out_specs=pl.BlockSpec((tm, B_WIDTH), lambda i: (i, 0)),
        out_shape=jax.ShapeDtypeStruct((t, B_WIDTH), F32),
        scratch_shapes=[pltpu.VMEM((B_WIDTH // S5_SLAB_LANES, tm, S5_SLAB_LANES), F32)],
        compiler_params=_cparams("parallel"),
        name="s5_glu",
    )(y2, proj2d, d_skip.reshape(1, B_WIDTH), w_glu_bf)


def _swa_kernel(sink_ref, q_ref, kp_ref, kc_ref, vp_ref, vc_ref, o_ref):
    first = pl.program_id(1) == 0
    w = WINDOW
    cols = GQA * w
    kpos = lax.broadcasted_iota(jnp.int32, (2 * w, cols), 0)
    qpos = lax.broadcasted_iota(jnp.int32, (2 * w, cols), 1) % w + w
    valid = (kpos <= qpos) & (qpos - kpos < w) & jnp.logical_not(first & (kpos < w))
    colhead = lax.broadcasted_iota(jnp.int32, (1, cols), 1) // w
    qt = q_ref[...].T * (HEAD_DIM ** -0.5 * LOG2E)
    vt = jnp.concatenate([vp_ref[...].T, vc_ref[...].T], axis=1)
    ones_rows = (lax.broadcasted_iota(jnp.int32, (BF16_SUBLANES, 2 * w), 0) == 0).astype(BF16)
    outs = []
    for kv in range(C_KV):
        ls = slice(kv * HEAD_DIM, (kv + 1) * HEAD_DIM)
        k2 = jnp.concatenate([kp_ref[:, ls], kc_ref[:, ls]], axis=0).astype(BF16)
        q_t = jnp.concatenate([qt[(kv * GQA + g) * HEAD_DIM:(kv * GQA + g + 1) * HEAD_DIM, :]
                               for g in range(GQA)], axis=1).astype(BF16)
        s = jnp.where(valid, _dot(k2, q_t), NEG)
        sink = jnp.zeros((1, cols), F32)
        for g in range(GQA):
            sink = jnp.where(colhead == g, sink_ref[kv * GQA + g] * LOG2E, sink)
        m = jnp.maximum(jnp.max(s, axis=0, keepdims=True), sink)
        p = jnp.exp2(s - m)
        vt_aug = jnp.concatenate([vt[ls, :].astype(BF16), ones_rows], axis=0)
        acc = _dot(vt_aug, p.astype(BF16))
        o_t = acc[0:HEAD_DIM, :] / (acc[HEAD_DIM:HEAD_DIM + 1, :] + jnp.exp2(sink - m))
        outs.extend(o_t[:, g * w:(g + 1) * w] for g in range(GQA))
    o_ref[...] = jnp.concatenate(outs, axis=0).T


def _swa(proj, sinks):
    bsz, l, _ = proj.shape
    w = WINDOW
    qw = C_HEADS * HEAD_DIM
    kvw = C_KV * HEAD_DIM
    kcol = qw // kvw
    prev = lambda b, n: (b, jnp.maximum(n - 1, 0), kcol)
    prev_v = lambda b, n: (b, jnp.maximum(n - 1, 0), kcol + 1)
    return pl.pallas_call(
        _swa_kernel,
        grid=(bsz, l // w),
        in_specs=[pl.BlockSpec(memory_space=pltpu.SMEM),
                  pl.BlockSpec((None, w, qw), lambda b, n: (b, n, 0)),
                  pl.BlockSpec((None, w, kvw), prev),
                  pl.BlockSpec((None, w, kvw), lambda b, n: (b, n, kcol)),
                  pl.BlockSpec((None, w, kvw), prev_v),
                  pl.BlockSpec((None, w, kvw), lambda b, n: (b, n, kcol + 1))],
        out_specs=pl.BlockSpec((None, w, qw), lambda b, n: (b, n, 0)),
        out_shape=jax.ShapeDtypeStruct((bsz, l, qw), F32),
        compiler_params=_cparams("parallel", "arbitrary"),
        name="swa",
    )(sinks.astype(F32), proj, proj, proj, proj, proj)


def _moba_prep_kernel(k_ref, v_ref, km_ref, ka_ref, vt_ref):
    j = pl.program_id(1)
    k = k_ref[...]
    km_ref[pl.ds(j, 1), :] = jnp.mean(k, axis=0, keepdims=True)
    nbp = ka_ref.shape[-1] - HEAD_DIM
    onehot = (lax.broadcasted_iota(jnp.int32, (k.shape[0], nbp), 1) == j).astype(BF16)
    vt = v_ref[...].T
    ones_rows = (lax.broadcasted_iota(jnp.int32, (BF16_SUBLANES, k.shape[0]), 0) == 0).astype(BF16)
    for kv in range(D_KV):
        ls = slice(kv * HEAD_DIM, (kv + 1) * HEAD_DIM)
        ka_ref[kv] = jnp.concatenate([k[:, ls].astype(BF16), onehot], axis=1)
        vt_ref[kv] = jnp.concatenate([vt[ls, :].astype(BF16), ones_rows], axis=0)


def _moba_kernel(q_ref, k_ref, vt_ref, km_ref, o_ref, qaug_ref, acc_ref, sa_ref, sb_ref):
    kvh = pl.program_id(1)
    i = pl.program_id(2)
    blk = MOBA_BLOCK
    cols = GQA * blk
    nb = km_ref.shape[0]
    scale = HEAD_DIM ** -0.5
    qt = q_ref[...].T * scale
    q_t = jnp.concatenate([qt[g * HEAD_DIM:(g + 1) * HEAD_DIM, :] for g in range(GQA)], axis=1)
    q_hi, q_lo = _split_bf16(q_t)

    km = jnp.where(kvh == 0, km_ref[:, :HEAD_DIM], km_ref[:, HEAD_DIM:])
    km_hi, km_lo = _split_bf16(km)
    gate = _dot(km_hi, q_hi) + _dot(km_hi, q_lo) + _dot(km_lo, q_hi)
    blkid = lax.broadcasted_iota(jnp.int32, (nb, cols), 0).astype(F32)
    past = blkid < i.astype(F32)
    gate = jnp.where(past, gate, NEG)
    sel = jnp.zeros((nb, cols), F32)
    for _ in range(min(MOBA_TOPK, nb)):
        top = jnp.max(gate, axis=0, keepdims=True)
        first = jnp.min(jnp.where(gate == top, blkid, float(nb)), axis=0, keepdims=True)
        hit = blkid == first
        sel = jnp.where(hit & past, 1.0, sel)
        gate = jnp.where(hit, -jnp.inf, gate)
    q2 = (q_t * LOG2E).astype(BF16)
    qaug_ref[0:HEAD_DIM, :] = q2
    pad = qaug_ref.shape[0] - HEAD_DIM - nb
    mask = jnp.where(sel > 0.0, 0.0, NEG)
    if pad:
        mask = jnp.concatenate([mask, jnp.zeros((pad, cols), F32)], axis=0)
    qaug_ref[HEAD_DIM:, :] = mask.astype(BF16)

    def k_rows(j):
        return k_ref[pl.ds(pl.multiple_of(j * blk, blk), blk), :]

    def scores(j):
        return _dot(k_rows(j), qaug_ref[...])

    s = _dot(k_rows(i)[:, :HEAD_DIM], q2)
    kpos = lax.broadcasted_iota(jnp.int32, (blk, cols), 0)
    qpos = lax.broadcasted_iota(jnp.int32, (blk, cols), 1) % blk
    s = jnp.where(kpos <= qpos, s, NEG)
    m0 = jnp.max(s, axis=0, keepdims=True)
    p = jnp.exp2(s - m0)
    acc_ref[...] = _dot(vt_ref[i], p.astype(BF16))

    def stage(s_ref, j):
        s_j = scores(j)
        s_ref[...] = s_j
        return jnp.max(s_j, axis=0, keepdims=True)

    def absorb(s_ref, s_max, j, m_old):
        m_new = jnp.maximum(m_old, s_max)
        p_j = jnp.exp2(s_ref[...] - m_new)
        corr = jnp.exp2(m_old - m_new)
        acc_ref[...] = acc_ref[...] * corr + _dot(vt_ref[j], p_j.astype(BF16))
        return m_new

    def body(t, carry):
        m, max_cur = carry
        j0 = MOBA_TRIP * t
        bufs = (sa_ref, sb_ref)
        for u in range(MOBA_TRIP):
            max_next = stage(bufs[(u + 1) % 2], jnp.minimum(j0 + u + 1, nb - 1))
            m = absorb(bufs[u % 2], max_cur, j0 + u, m)
            max_cur = max_next
        return m, max_cur

    lax.fori_loop(0, (i + MOBA_TRIP - 1) // MOBA_TRIP, body, (m0, stage(sa_ref, 0)))
    out_t = acc_ref[0:HEAD_DIM, :] / acc_ref[HEAD_DIM:HEAD_DIM + 1, :]
    stacked = jnp.concatenate([out_t[:, g * blk:(g + 1) * blk] for g in range(GQA)], axis=0)
    o_ref[...] = stacked.T


def _moba(proj):
    bsz, l, _ = proj.shape
    blk = MOBA_BLOCK
    nb = l // blk
    kvw = D_KV * HEAD_DIM
    qgw = GQA * HEAD_DIM
    q0 = C_HEADS * HEAD_DIM + 2 * C_KV * HEAD_DIM
    k0 = q0 + D_HEADS * HEAD_DIM
    qcol0 = q0 // qgw
    nbp = -(-nb // BF16_SUBLANES) * BF16_SUBLANES
    vh = HEAD_DIM + BF16_SUBLANES
    cols = GQA * blk
    kw = HEAD_DIM + nbp
    kmean, k_aug, vt_bf = pl.pallas_call(
        _moba_prep_kernel,
        grid=(bsz, nb),
        in_specs=[pl.BlockSpec((None, blk, kvw), lambda b, j: (b, j, k0 // kvw)),
                  pl.BlockSpec((None, blk, kvw), lambda b, j: (b, j, k0 // kvw + 1))],
        out_specs=[pl.BlockSpec((None, nb, kvw), lambda b, j: (b, 0, 0)),
                   pl.BlockSpec((None, D_KV, blk, kw), lambda b, j: (b, 0, j, 0)),
                   pl.BlockSpec((None, D_KV, None, vh, blk), lambda b, j: (b, 0, j, 0, 0))],
        out_shape=[jax.ShapeDtypeStruct((bsz, nb, kvw), F32),
                   jax.ShapeDtypeStruct((bsz, D_KV, l, kw), BF16),
                   jax.ShapeDtypeStruct((bsz, D_KV, nb, vh, blk), BF16)],
        compiler_params=_cparams("parallel", "arbitrary"),
        name="moba_prep",
    )(proj, proj)
    return pl.pallas_call(
        _moba_kernel,
        grid=(bsz, D_KV, nb),
        in_specs=[pl.BlockSpec((None, blk, qgw), lambda b, h, i: (b, i, qcol0 + h)),
                  pl.BlockSpec((None, None, l, kw), lambda b, h, i: (b, h, 0, 0)),
                  pl.BlockSpec((None, None, nb, vh, blk), lambda b, h, i: (b, h, 0, 0, 0)),
                  pl.BlockSpec((None, nb, kvw), lambda b, h, i: (b, 0, 0))],
        out_specs=pl.BlockSpec((None, blk, qgw), lambda b, h, i: (b, i, h)),
        out_shape=jax.ShapeDtypeStruct((bsz, l, D_HEADS * HEAD_DIM), F32),
        scratch_shapes=[pltpu.VMEM((kw, cols), BF16), pltpu.VMEM((vh, cols), F32),
                        pltpu.VMEM((blk, cols), F32), pltpu.VMEM((blk, cols), F32)],
        compiler_params=_cparams("parallel", "arbitrary", "arbitrary"),
        name="moba",
    )(proj, k_aug, vt_bf, kmean)


ROUTER_LANES = 128
MOE_EXPERTS_PER_STEP = 2


def _router_logits(x, wr_ref, br_ref):
    x_hi, x_lo = _split_bf16(x)
    w_hi, w_lo = _split_bf16(wr_ref[...])
    return _dot(x_hi, w_hi) + _dot(x_lo, w_hi) + _dot(x_hi, w_lo) + br_ref[...]


def _lane_ids(shape):
    return lax.broadcasted_iota(jnp.int32, shape, 1).astype(F32)


def _top_group(logits):
    lane = _lane_ids(logits.shape)
    gl = jnp.where(lane < N_GROUPS, logits, -jnp.inf)
    gmax = jnp.max(gl, axis=-1, keepdims=True)
    return jnp.min(jnp.where(gl == gmax, lane, float(ROUTER_LANES)), axis=-1, keepdims=True)


def _moe_gates(logits, g_idx):
    lane = _lane_ids(logits.shape)
    big = float(ROUTER_LANES)
    gl = jnp.where(lane < N_GROUPS, logits, -jnp.inf)
    gmax = jnp.max(gl, axis=-1, keepdims=True)
    g_logit = jnp.sum(jnp.where(lane == g_idx, logits, 0.0), axis=-1, keepdims=True)
    g_top = jnp.exp(g_logit - gmax) / jnp.sum(jnp.exp(gl - gmax), axis=-1, keepdims=True)
    lo = N_GROUPS + EXP_PER_GROUP * g_idx
    el = jnp.where((lane >= lo) & (lane < lo + EXP_PER_GROUP), logits, -jnp.inf)
    m1 = jnp.max(el, axis=-1, keepdims=True)
    i1 = jnp.min(jnp.where(el == m1, lane, big), axis=-1, keepdims=True)
    el2 = jnp.where(lane == i1, -jnp.inf, el)
    m2 = jnp.max(el2, axis=-1, keepdims=True)
    i2 = jnp.min(jnp.where(el2 == m2, lane, big), axis=-1, keepdims=True)
    e21 = jnp.exp(m2 - m1)
    w1 = g_top / (1.0 + e21)
    w2 = w1 * e21
    return jnp.where(lane == i1, w1, 0.0) + jnp.where(lane == i2, w2, 0.0)


def _moe_kernel(x_ref, wr_ref, br_ref, wgu_ref, wd_ref, g_ref, b_ref, o_ref, gates_ref, xb_ref, acc_ref, *, alpha):
    e = pl.program_id(1)

    @pl.when(e == 0)
    def _():
        x = x_ref[...]
        logits = _router_logits(x, wr_ref, br_ref)
        gates_ref[...] = _moe_gates(logits, _top_group(logits))
        xb_ref[...] = x.astype(BF16)
        acc_ref[...] = jnp.zeros_like(acc_ref)

    gates = gates_ref[...]
    lane = lax.broadcasted_iota(jnp.int32, gates.shape, 1)
    xb = xb_ref[...]
    total = None
    for u in range(MOE_EXPERTS_PER_STEP):
        gate_e = jnp.sum(jnp.where(lane == N_GROUPS + e * MOE_EXPERTS_PER_STEP + u, gates, 0.0),
                         axis=-1, keepdims=True)
        hu = _dot(xb, wgu_ref[u].astype(BF16))
        h = _silu(hu[:, :EXP_HIDDEN]) * hu[:, EXP_HIDDEN:] * gate_e
        part = _dot(h.astype(BF16), wd_ref[u].astype(BF16))
        total = part if total is None else total + part
    acc_ref[...] += total

    @pl.when(e == N_EXPERTS // MOE_EXPERTS_PER_STEP - 1)
    def _():
        o_ref[...] = _layer_norm_rows(alpha * x_ref[...] + acc_ref[...], g_ref[...], b_ref[...])


def _moe_ln(x2d, w_group, b_group, w_expert, b_expert, w_gate_up, w_down, g, b, alpha, layer=0, tm=1024):
    t, d = x2d.shape
    w_gate_up = w_gate_up.reshape((-1,) + w_gate_up.shape[-3:])
    w_down = w_down.reshape((-1,) + w_down.shape[-3:])
    eps = MOE_EXPERTS_PER_STEP
    tm = min(tm, t)
    pad = ROUTER_LANES - N_GROUPS - N_EXPERTS
    w_router = jnp.concatenate([w_group, w_expert, jnp.zeros((d, pad), F32)], axis=1).astype(F32)
    b_router = jnp.concatenate([b_group, b_expert, jnp.zeros((pad,), F32)]).astype(F32).reshape(1, ROUTER_LANES)
    return pl.pallas_call(
        functools.partial(_moe_kernel, alpha=alpha),
        grid=(t // tm, N_EXPERTS // eps),
        in_specs=[pl.BlockSpec((tm, d), lambda i, e: (i, 0)),
                  pl.BlockSpec((d, ROUTER_LANES), lambda i, e: (0, 0)),
                  pl.BlockSpec((1, ROUTER_LANES), lambda i, e: (0, 0)),
                  pl.BlockSpec((None, eps, d, 2 * EXP_HIDDEN), lambda i, e: (layer, e, 0, 0)),
                  pl.BlockSpec((None, eps, EXP_HIDDEN, d), lambda i, e: (layer, e, 0, 0)),
                  pl.BlockSpec((1, d), lambda i, e: (0, 0)),
                  pl.BlockSpec((1, d), lambda i, e: (0, 0))],
        out_specs=pl.BlockSpec((tm, d), lambda i, e: (i, 0)),
        out_shape=jax.ShapeDtypeStruct((t, d), F32),
        scratch_shapes=[pltpu.VMEM((tm, ROUTER_LANES), F32), pltpu.VMEM((tm, d), BF16),
                        pltpu.VMEM((tm, d), F32)],
        compiler_params=_cparams("parallel", "arbitrary"),
        name="moe_ln",
    )(x2d, w_router, b_router, w_gate_up, w_down, g.reshape(1, d), b.reshape(1, d))


def kernel(x, hgrn_lb_logits, ev_w_in, ev_a_norm, ev_s5_a_re, ev_s5_a_im, ev_s5_log_dt, ev_s5_b_re, ev_s5_b_im, ev_s5_c_re, ev_s5_c_im, ev_s5_d, ev_s5_w_glu, ev_w_out, od_w_in, od_sinks, od_w_out, ln1_g, ln1_b, moe_w_group, moe_b_group, moe_w_expert, moe_b_expert, moe_w_gate_up, moe_w_down, ln2_g, ln2_b):
    bsz, l, d = x.shape
    depth = ln1_g.shape[0]
    alpha = (2.0 * depth) ** 0.25
    t = bsz * l
    lower_bounds = jnp.cumsum(jax.nn.softmax(hgrn_lb_logits.astype(F32), axis=0), axis=0)
    x2d = x.reshape(t, d)
    for layer in range(depth):
        j = layer // 2
        if layer % 2 == 0:
            proj = _proj(x2d, ev_w_in[j].astype(BF16))
            proj3 = proj.reshape(bsz, l, proj.shape[1])
            ya = _hgrn2(proj3, lower_bounds[layer], ev_a_norm[j])
            y2 = _s5_core(proj, bsz, ev_s5_a_re[j], ev_s5_a_im[j], ev_s5_log_dt[j],
                          ev_s5_b_re[j], ev_s5_b_im[j], ev_s5_c_re[j], ev_s5_c_im[j])
            yb = _s5_glu(y2, proj, ev_s5_d[j], ev_s5_w_glu[j].astype(BF16))
            x2d = _mix_ln(ya.reshape(t, A_W), yb, x2d, ev_w_out[j].astype(BF16), ln1_g[layer], ln1_b[layer], alpha)
        else:
            proj = _proj(x2d, od_w_in[j].astype(BF16))
            proj3 = proj.reshape(bsz, l, proj.shape[1])
            yc = _swa(proj3, od_sinks[j])
            yd = _moba(proj3)
            x2d = _mix_ln(yc.reshape(t, -1), yd.reshape(t, -1), x2d, od_w_out[j].astype(BF16),
                          ln1_g[layer], ln1_b[layer], alpha)
        x2d = _moe_ln(x2d, moe_w_group[layer], moe_b_group[layer], moe_w_expert[layer], moe_b_expert[layer],
                      moe_w_gate_up, moe_w_down, ln2_g[layer], ln2_b[layer], alpha, layer=layer)
    return x2d.reshape(bsz, l, d)
```

```python
import functools
import math

import jax
import jax.numpy as jnp
from jax import lax
from jax.experimental import pallas as pl
from jax.experimental.pallas import tpu as pltpu

F32 = jnp.float32
BF16 = jnp.bfloat16
MIX_DTYPE = BF16
NEG = -1e30
LN_EPS = 1e-5
RMS_EPS = 1e-6

A_HEADS = 4
A_DK = 128
A_DV = 128
A_CHUNK = 64
A_W = A_HEADS * A_DK

B_WIDTH = 512
B_GROUP = 16
B_GROUPS = B_WIDTH // B_GROUP
B_STATE = 64
S5_LC = 16
S5_CW = S5_LC * B_GROUP
S5_SLAB_LANES = 128

HEAD_DIM = 64
C_HEADS = 8
C_KV = 2
WINDOW = 128
D_HEADS = 8
D_KV = 2
MOBA_BLOCK = 256
MOBA_TOPK = 3
MOBA_TRIP = 4
MOBA_EXP_HEADROOM = 60.0
LOG2E = math.log2(math.e)
BF16_SUBLANES = 16
GQA = C_HEADS // C_KV

N_GROUPS = 4
EXP_PER_GROUP = 4
N_EXPERTS = 16
EXP_HIDDEN = 256

VMEM_LIMIT_BYTES = 48 * 1024 * 1024


def _cparams(*sem):
    return pltpu.CompilerParams(dimension_semantics=sem, vmem_limit_bytes=VMEM_LIMIT_BYTES)


def _nt_dot(a, b):
    return lax.dot_general(a, b, (((1,), (1,)), ((), ())), preferred_element_type=F32)


def _dot(a, b):
    return jnp.dot(a, b, preferred_element_type=F32)


def _split_bf16(v):
    hi = v.astype(BF16)
    lo = (v - hi.astype(F32)).astype(BF16)
    return hi, lo


def _silu(v):
    return v * jax.nn.sigmoid(v)


def _layer_norm_rows(r, g, b):
    mu = jnp.mean(r, axis=-1, keepdims=True)
    d = r - mu
    var = jnp.mean(d * d, axis=-1, keepdims=True)
    return d * lax.rsqrt(var + LN_EPS) * g + b


def _proj_kernel(x_ref, w_ref, o_ref):
    o_ref[...] = _dot(x_ref[...].astype(BF16), w_ref[...])


def _proj(x2d, w_bf, tm=512):
    t, d = x2d.shape
    n = w_bf.shape[1]
    return pl.pallas_call(
        _proj_kernel,
        grid=(t // tm,),
        in_specs=[pl.BlockSpec((tm, d), lambda i: (i, 0)),
                  pl.BlockSpec((d, n), lambda i: (0, 0))],
        out_specs=pl.BlockSpec((tm, n), lambda i: (i, 0)),
        out_shape=jax.ShapeDtypeStruct((t, n), F32),
        compiler_params=_cparams("parallel"),
        name="in_proj",
    )(x2d, w_bf)


def _mix_ln_kernel(ya_ref, yb_ref, x_ref, w_ref, g_ref, b_ref, o_ref, *, alpha):
    half = ya_ref.shape[1]
    mix = _dot(ya_ref[...].astype(BF16), w_ref[0:half, :]) + _dot(yb_ref[...].astype(BF16), w_ref[half:, :])
    o_ref[...] = _layer_norm_rows(alpha * x_ref[...] + mix, g_ref[...], b_ref[...])


def _mix_ln(ya, yb, x2d, w_out_bf, g, b, alpha, tm=512):
    t, d = x2d.shape
    half = ya.shape[1]
    return pl.pallas_call(
        functools.partial(_mix_ln_kernel, alpha=alpha),
        grid=(t // tm,),
        in_specs=[pl.BlockSpec((tm, half), lambda i: (i, 0)),
                  pl.BlockSpec((tm, half), lambda i: (i, 0)),
                  pl.BlockSpec((tm, d), lambda i: (i, 0)),
                  pl.BlockSpec((2 * half, d), lambda i: (0, 0)),
                  pl.BlockSpec((1, d), lambda i: (0, 0)),
                  pl.BlockSpec((1, d), lambda i: (0, 0))],
        out_specs=pl.BlockSpec((tm, d), lambda i: (i, 0)),
        out_shape=jax.ShapeDtypeStruct((t, d), F32),
        compiler_params=_cparams("parallel"),
        name="mix_ln",
    )(ya, yb, x2d, w_out_bf, g.reshape(1, d), b.reshape(1, d))


def _hgrn2_kernel(proj_ref, lb_ref, ng_ref, tri_ref, o_ref, st_ref, *, tb):
    @pl.when(pl.program_id(1) == 0)
    def _():
        st_ref[...] = jnp.zeros_like(st_ref)

    ql = proj_ref[:, 0:A_W]
    fl = proj_ref[:, A_W:2 * A_W]
    iv = proj_ref[:, 2 * A_W:3 * A_W]
    gt = proj_ref[:, 3 * A_W:4 * A_W]
    lb = lb_ref[...]
    f = lb + (1.0 - lb) * jax.nn.sigmoid(fl)
    k = 1.0 - f
    lf = jnp.log(f)
    lf_hi, lf_lo = _split_bf16(lf)
    tri = tri_ref[...]
    b = _dot(tri, lf_hi) + _dot(tri, lf_lo)
    qd = _silu(ql) * jnp.exp(b)
    ki = k * jnp.exp(-b)
    og = ng_ref[...] * _silu(gt)

    row = lax.broadcasted_iota(jnp.int32, (A_CHUNK, A_CHUNK), 0)
    col = lax.broadcasted_iota(jnp.int32, (A_CHUNK, A_CHUNK), 1)
    causal = col <= row
    v_t = [iv[:, h * A_DV:(h + 1) * A_DV].T for h in range(A_HEADS)]

    for c in range(tb // A_CHUNK):
        r0 = c * A_CHUNK
        b_c = b[r0:r0 + A_CHUNK, :]
        b_last = b[r0 + A_CHUNK - 1:r0 + A_CHUNK, :]
        kt = k[r0:r0 + A_CHUNK, :] * jnp.exp(b_last - b_c)
        dec = jnp.exp(b_last)
        for h in range(A_HEADS):
            ls = slice(h * A_DK, (h + 1) * A_DK)
            qd_h = qd[r0:r0 + A_CHUNK, ls].astype(BF16)
            ki_h = ki[r0:r0 + A_CHUNK, ls].astype(BF16)
            v_h = iv[r0:r0 + A_CHUNK, ls]
            att = jnp.where(causal, _nt_dot(qd_h, ki_h), 0.0)
            s_t = st_ref[h]
            o = _dot(att.astype(BF16), v_h.astype(BF16)) + _nt_dot(qd_h, s_t.astype(BF16))
            st_ref[h] = s_t * dec[:, ls] + _dot(v_t[h][:, r0:r0 + A_CHUNK].astype(BF16), kt[:, ls].astype(BF16))
            o = o * lax.rsqrt(jnp.mean(o * o, axis=-1, keepdims=True) + RMS_EPS)
            o_ref[r0:r0 + A_CHUNK, ls] = (o * og[r0:r0 + A_CHUNK, ls]).astype(o_ref.dtype)


def _hgrn2(proj, lb, norm_g, tb=256):
    bsz, l, _ = proj.shape
    tb = min(tb, l)
    idx = jnp.arange(tb)
    tri = ((idx[:, None] >= idx[None, :]) & (idx[:, None] // A_CHUNK == idx[None, :] // A_CHUNK)).astype(BF16)
    return pl.pallas_call(
        functools.partial(_hgrn2_kernel, tb=tb),
        grid=(bsz, l // tb),
        in_specs=[pl.BlockSpec((None, tb, 4 * A_W), lambda b, t: (b, t, 0)),
                  pl.BlockSpec((1, A_W), lambda b, t: (0, 0)),
                  pl.BlockSpec((1, A_W), lambda b, t: (0, 0)),
                  pl.BlockSpec((tb, tb), lambda b, t: (0, 0))],
        out_specs=pl.BlockSpec((None, tb, A_W), lambda b, t: (b, t, 0)),
        out_shape=jax.ShapeDtypeStruct((bsz, l, A_W), MIX_DTYPE),
        scratch_shapes=[pltpu.VMEM((A_HEADS, A_DV, A_DK), F32)],
        compiler_params=_cparams("parallel", "arbitrary"),
        name="hgrn2",
    )(proj, lb.reshape(1, A_W), norm_g.reshape(1, A_W), tri)


def _s5_weights(a_re, a_im, log_dt, b_re, b_im, c_re, c_im):
    g, p = a_re.shape
    lc = S5_LC
    dt = jnp.exp(log_dt.astype(F32))[:, None]
    ar, ai = a_re.astype(F32), a_im.astype(F32)
    mag = jnp.exp(dt * ar)
    abar_re, abar_im = mag * jnp.cos(dt * ai), mag * jnp.sin(dt * ai)
    den = ar * ar + ai * ai
    xr, xi = abar_re - 1.0, abar_im
    fr = (xr * ar + xi * ai) / den
    fi = (xi * ar - xr * ai) / den
    br, bi = b_re.astype(F32), b_im.astype(F32)
    bb_re = fr[..., None] * br - fi[..., None] * bi
    bb_im = fr[..., None] * bi + fi[..., None] * br
    cr, ci = c_re.astype(F32), c_im.astype(F32)
    tau = jnp.arange(lc + 1, dtype=F32)[:, None, None]
    pmag = jnp.exp(tau * (dt * ar)[None])
    pw_re = pmag * jnp.cos(tau * (dt * ai)[None])
    pw_im = pmag * jnp.sin(tau * (dt * ai)[None])
    hp = lax.Precision.HIGHEST
    ab_re = pw_re[..., None] * bb_re[None] - pw_im[..., None] * bb_im[None]
    ab_im = pw_re[..., None] * bb_im[None] + pw_im[..., None] * bb_re[None]
    kk = (jnp.einsum('gnp,tgpm->tgnm', cr, ab_re, precision=hp)
          - jnp.einsum('gnp,tgpm->tgnm', ci, ab_im, precision=hp))
    s_idx = jnp.arange(lc)[:, None]
    t_idx = jnp.arange(lc)[None, :]
    lag = t_idx - s_idx
    toe = jnp.where((lag >= 0)[:, :, None, None, None], kk[jnp.clip(lag, 0, lc)], 0.0)
    toe = toe.transpose(2, 0, 4, 1, 3).reshape(g, lc * B_GROUP, lc * B_GROUP)
    rev = lc - 1 - jnp.arange(lc)
    wx_re = ab_re[rev].transpose(1, 0, 3, 2).reshape(g, lc * B_GROUP, p)
    wx_im = ab_im[rev].transpose(1, 0, 3, 2).reshape(g, lc * B_GROUP, p)
    pr, pi_ = pw_re[1:], pw_im[1:]
    wy_re = cr[None] * pr[:, :, None, :] - ci[None] * pi_[:, :, None, :]
    wy_im = -(cr[None] * pi_[:, :, None, :] + ci[None] * pr[:, :, None, :])
    wy_re = wy_re.transpose(1, 3, 0, 2).reshape(g, p, lc * B_GROUP)
    wy_im = wy_im.transpose(1, 3, 0, 2).reshape(g, p, lc * B_GROUP)
    return toe, wx_re, wx_im, wy_re, wy_im, pw_re[lc], pw_im[lc]


def _pair_pack_rows(w):
    g, r, c = w.shape
    w = w.reshape(g // 2, 2, r, c)
    z = jnp.zeros((g // 2, r, c), w.dtype)
    top = jnp.concatenate([w[:, 0], z], axis=2)
    bot = jnp.concatenate([z, w[:, 1]], axis=2)
    return jnp.concatenate([top, bot], axis=1)


def _s5_inject_kernel(u_ref, wre_ref, wim_ref, xre_ref, xim_ref):
    lhs = jnp.concatenate([u_ref[0], u_ref[1]], axis=1)
    xre_ref[...] = _dot(lhs, wre_ref[...])
    xim_ref[...] = _dot(lhs, wim_ref[...])


def _s5_scan_kernel(xre_ref, xim_ref, are_ref, aim_ref, hre_ref, him_ref, sre_ref, sim_ref, *, tcs):
    @pl.when(pl.program_id(0) == 0)
    def _():
        sre_ref[...] = jnp.zeros_like(sre_ref)
        sim_ref[...] = jnp.zeros_like(sim_ref)

    ar = are_ref[...]
    ai = aim_ref[...]

    def body(r, carry):
        hr, hi = carry
        hre_ref[r] = hr
        him_ref[r] = hi
        return ar * hr - ai * hi + xre_ref[r], ar * hi + ai * hr + xim_ref[r]

    hr, hi = lax.fori_loop(0, tcs, body, (sre_ref[...], sim_ref[...]), unroll=8)
    sre_ref[...] = hr
    sim_ref[...] = hi


def _s5_readout_kernel(u_ref, toe_ref, wyre_ref, wyim_ref, hre_ref, him_ref, y_ref):
    hre = hre_ref[...].astype(BF16)
    him = him_ref[...].astype(BF16)
    for r in range(2):
        y_ref[r] = (_dot(u_ref[r], toe_ref[r])
                    + _dot(hre, wyre_ref[r]) + _dot(him, wyim_ref[r]))


def _s5_pack_kernel(*refs):
    u_refs, o_ref = refs[:-1], refs[-1]
    tc = o_ref.shape[1]
    gps = S5_SLAB_LANES // B_GROUP
    for r, u_ref in enumerate(u_refs):
        rows = [u_ref[pl.ds(s, tc, stride=S5_LC), :] for s in range(S5_LC)]
        for j in range(gps):
            piece = jnp.concatenate([row[:, j * B_GROUP:(j + 1) * B_GROUP] for row in rows], axis=1)
            o_ref[r * gps + j] = piece.astype(BF16)


def _s5_pack(proj2d, tc=128):
    t, wd = proj2d.shape
    nc = t // S5_LC
    tc = min(tc, nc)
    col0 = (wd - B_WIDTH) // S5_SLAB_LANES
    nslab = B_WIDTH // S5_SLAB_LANES
    return pl.pallas_call(
        _s5_pack_kernel,
        grid=(nc // tc,),
        in_specs=[pl.BlockSpec((tc * S5_LC, S5_SLAB_LANES), functools.partial(lambda i, r: (i, col0 + r), r=r))
                  for r in range(nslab)],
        out_specs=pl.BlockSpec((B_GROUPS, tc, S5_CW), lambda i: (0, i, 0)),
        out_shape=jax.ShapeDtypeStruct((B_GROUPS, nc, S5_CW), BF16),
        compiler_params=_cparams("parallel"),
        name="s5_pack",
    )(*([proj2d] * nslab))


def _s5_core(proj2d, bsz, a_re, a_im, log_dt, b_re, b_im, c_re, c_im):
    l = proj2d.shape[0] // bsz
    g, p, lc, cw = B_GROUPS, B_STATE, S5_LC, S5_CW
    ncb = l // lc
    toe, wx_re, wx_im, wy_re, wy_im, a_re_lc, a_im_lc = _s5_weights(a_re, a_im, log_dt, b_re, b_im, c_re, c_im)
    toe = toe.astype(BF16)
    wx_re_p = _pair_pack_rows(wx_re).astype(BF16)
    wx_im_p = _pair_pack_rows(wx_im).astype(BF16)
    zero = jnp.zeros_like(wy_re)
    even = (jnp.arange(g) % 2 == 0)[:, None, None]
    wy_re_p = jnp.where(even, jnp.concatenate([wy_re, zero], 1), jnp.concatenate([zero, wy_re], 1)).astype(BF16)
    wy_im_p = jnp.where(even, jnp.concatenate([wy_im, zero], 1), jnp.concatenate([zero, wy_im], 1)).astype(BF16)

    u2 = _s5_pack(proj2d)
    tc = min(1024, ncb)
    nct = ncb // tc
    slab = 2 * p
    x_shape = jax.ShapeDtypeStruct((ncb, bsz * g * p), F32)
    x_spec = pl.BlockSpec((tc, slab), lambda j, b, i: (i, b * (g // 2) + j))
    u_spec = pl.BlockSpec((2, tc, cw), lambda j, b, i: (j, b * nct + i, 0))
    xre, xim = pl.pallas_call(
        _s5_inject_kernel,
        grid=(g // 2, bsz, nct),
        in_specs=[u_spec,
                  pl.BlockSpec((None, 2 * cw, slab), lambda j, b, i: (j, 0, 0)),
                  pl.BlockSpec((None, 2 * cw, slab), lambda j, b, i: (j, 0, 0))],
        out_specs=[x_spec, x_spec],
        out_shape=[x_shape, x_shape],
        compiler_params=_cparams("parallel", "parallel", "parallel"),
        name="s5_inject",
    )(u2, wx_re_p, wx_im_p)

    lanes = 512
    rows = bsz * g * p // lanes
    tcs = min(128, ncb)
    a_tile = lambda a: jnp.tile(a.reshape(g * p // lanes, lanes), (bsz, 1))
    seq_spec = pl.BlockSpec((tcs, rows, lanes), lambda i: (i, 0, 0))
    par_spec = pl.BlockSpec((rows, lanes), lambda i: (0, 0))
    seq_shape = jax.ShapeDtypeStruct((ncb, rows, lanes), F32)
    hre, him = pl.pallas_call(
        functools.partial(_s5_scan_kernel, tcs=tcs),
        grid=(ncb // tcs,),
        in_specs=[seq_spec, seq_spec, par_spec, par_spec],
        out_specs=[seq_spec, seq_spec],
        out_shape=[seq_shape, seq_shape],
        scratch_shapes=[pltpu.VMEM((rows, lanes), F32), pltpu.VMEM((rows, lanes), F32)],
        compiler_params=_cparams("arbitrary"),
        name="s5_scan",
    )(xre.reshape(ncb, rows, lanes), xim.reshape(ncb, rows, lanes), a_tile(a_re_lc), a_tile(a_im_lc))
    hre = hre.reshape(ncb, bsz * g * p)
    him = him.reshape(ncb, bsz * g * p)

    y2 = pl.pallas_call(
        _s5_readout_kernel,
        grid=(g // 2, bsz, nct),
        in_specs=[u_spec,
                  pl.BlockSpec((2, cw, cw), lambda j, b, i: (j, 0, 0)),
                  pl.BlockSpec((2, slab, cw), lambda j, b, i: (j, 0, 0)),
                  pl.BlockSpec((2, slab, cw), lambda j, b, i: (j, 0, 0)),
                  x_spec, x_spec],
        out_specs=u_spec,
        out_shape=jax.ShapeDtypeStruct((g, bsz * ncb, cw), F32),
        compiler_params=_cparams("parallel", "parallel", "parallel"),
        name="s5_readout",
    )(u2, toe, wy_re_p, wy_im_p, hre, him)
    return y2


def _s5_glu_kernel(y2_ref, u_ref, d_ref, w_ref, o_ref, yc_ref):
    tc = y2_ref.shape[1]
    gps = S5_SLAB_LANES // B_GROUP
    for r in range(yc_ref.shape[0]):
        grp = [y2_ref[r * gps + j] for j in range(gps)]
        for s in range(S5_LC):
            yc_ref[r, pl.ds(s, tc, stride=S5_LC), :] = jnp.concatenate(
                [blk[:, s * B_GROUP:(s + 1) * B_GROUP] for blk in grp], axis=1)
    yc = jnp.concatenate([yc_ref[r] for r in range(yc_ref.shape[0])], axis=1)
    y = jax.nn.gelu(yc + d_ref[...] * u_ref[...])
    z = _dot(y.astype(BF16), w_ref[...])
    o_ref[...] = (z[:, :B_WIDTH] * jax.nn.sigmoid(z[:, B_WIDTH:])).astype(o_ref.dtype)


def _s5_glu(y2, proj2d, d_skip, w_glu_bf, tc=64):
    t = proj2d.shape[0]
    nc = t // S5_LC
    tc = min(tc, nc)
    tm = tc * S5_LC
    ucol = proj2d.shape[1] // B_WIDTH - 1
    return pl.pallas_call(
        _s5_glu_kernel,
        grid=(nc // tc,),
        in_specs=[pl.BlockSpec((B_GROUPS, tc, S5_CW), lambda i: (0, i, 0)),
                  pl.BlockSpec((tm, B_WIDTH), lambda i: (i, ucol)),
                  pl.BlockSpec((1, B_WIDTH), lambda i: (0, 0)),
                  pl.BlockSpec((B_WIDTH, 2 * B_WIDTH), lambda i: (0, 0))],
        out_specs=pl.BlockSpec((tm, B_WIDTH), lambda i: (i, 0)),
        out_shape=jax.ShapeDtypeStruct((t, B_WIDTH), MIX_DTYPE),
        scratch_shapes=[pltpu.VMEM((B_WIDTH // S5_SLAB_LANES, tm, S5_SLAB_LANES), F32)],
        compiler_params=_cparams("parallel"),
        name="s5_glu",
    )(y2, proj2d, d_skip.reshape(1, B_WIDTH), w_glu_bf)


def _swa_kernel(sink_ref, q_ref, kp_ref, kc_ref, vp_ref, vc_ref, o_ref):
    first = pl.program_id(1) == 0
    w = WINDOW
    cols = GQA * w
    kpos = lax.broadcasted_iota(jnp.int32, (2 * w, cols), 0)
    qpos = lax.broadcasted_iota(jnp.int32, (2 * w, cols), 1) % w + w
    valid = (kpos <= qpos) & (qpos - kpos < w) & jnp.logical_not(first & (kpos < w))
    colhead = lax.broadcasted_iota(jnp.int32, (1, cols), 1) // w
    qt = q_ref[...].T * (HEAD_DIM ** -0.5 * LOG2E)
    vt = jnp.concatenate([vp_ref[...].T, vc_ref[...].T], axis=1)
    ones_rows = (lax.broadcasted_iota(jnp.int32, (BF16_SUBLANES, 2 * w), 0) == 0).astype(BF16)
    outs = []
    for kv in range(C_KV):
        ls = slice(kv * HEAD_DIM, (kv + 1) * HEAD_DIM)
        k2 = jnp.concatenate([kp_ref[:, ls], kc_ref[:, ls]], axis=0).astype(BF16)
        q_t = jnp.concatenate([qt[(kv * GQA + g) * HEAD_DIM:(kv * GQA + g + 1) * HEAD_DIM, :]
                               for g in range(GQA)], axis=1).astype(BF16)
        s = jnp.where(valid, _dot(k2, q_t), NEG)
        sink = jnp.zeros((1, cols), F32)
        for g in range(GQA):
            sink = jnp.where(colhead == g, sink_ref[kv * GQA + g] * LOG2E, sink)
        m = jnp.maximum(jnp.max(s, axis=0, keepdims=True), sink)
        p = jnp.exp2(s - m)
        vt_aug = jnp.concatenate([vt[ls, :].astype(BF16), ones_rows], axis=0)
        acc = _dot(vt_aug, p.astype(BF16))
        o_t = acc[0:HEAD_DIM, :] / (acc[HEAD_DIM:HEAD_DIM + 1, :] + jnp.exp2(sink - m))
        outs.extend(o_t[:, g * w:(g + 1) * w] for g in range(GQA))
    o_ref[...] = jnp.concatenate(outs, axis=0).T.astype(o_ref.dtype)


def _swa(proj, sinks):
    bsz, l, _ = proj.shape
    w = WINDOW
    qw = C_HEADS * HEAD_DIM
    kvw = C_KV * HEAD_DIM
    kcol = qw // kvw
    prev = lambda b, n: (b, jnp.maximum(n - 1, 0), kcol)
    prev_v = lambda b, n: (b, jnp.maximum(n - 1, 0), kcol + 1)
    return pl.pallas_call(
        _swa_kernel,
        grid=(bsz, l // w),
        in_specs=[pl.BlockSpec(memory_space=pltpu.SMEM),
                  pl.BlockSpec((None, w, qw), lambda b, n: (b, n, 0)),
                  pl.BlockSpec((None, w, kvw), prev),
                  pl.BlockSpec((None, w, kvw), lambda b, n: (b, n, kcol)),
                  pl.BlockSpec((None, w, kvw), prev_v),
                  pl.BlockSpec((None, w, kvw), lambda b, n: (b, n, kcol + 1))],
        out_specs=pl.BlockSpec((None, w, qw), lambda b, n: (b, n, 0)),
        out_shape=jax.ShapeDtypeStruct((bsz, l, qw), MIX_DTYPE),
        compiler_params=_cparams("parallel", "arbitrary"),
        name="swa",
    )(sinks.astype(F32), proj, proj, proj, proj, proj)


def _moba_prep_kernel(k_ref, v_ref, km_ref, kn_ref, ka_ref, vt_ref):
    j = pl.program_id(1)
    k = k_ref[...]
    km_ref[pl.ds(j, 1), :] = jnp.mean(k, axis=0, keepdims=True)
    kr = k.astype(BF16).astype(F32)
    kr2 = kr * kr
    lane = lax.broadcasted_iota(jnp.int32, (1, k.shape[1]), 1)
    norms = [jnp.sqrt(jnp.max(jnp.sum(kr2[:, kv * HEAD_DIM:(kv + 1) * HEAD_DIM], axis=1, keepdims=True),
                              axis=0, keepdims=True)) for kv in range(D_KV)]
    kn_ref[pl.ds(j, 1), :] = jnp.where(lane < HEAD_DIM, norms[0], norms[1])
    nbp = ka_ref.shape[-1] - HEAD_DIM
    onehot = (lax.broadcasted_iota(jnp.int32, (k.shape[0], nbp), 1) == j).astype(BF16)
    vt = v_ref[...].T
    ones_rows = (lax.broadcasted_iota(jnp.int32, (BF16_SUBLANES, k.shape[0]), 0) == 0).astype(BF16)
    for kv in range(D_KV):
        ls = slice(kv * HEAD_DIM, (kv + 1) * HEAD_DIM)
        ka_ref[kv] = jnp.concatenate([k[:, ls].astype(BF16), onehot], axis=1)
        vt_ref[kv] = jnp.concatenate([vt[ls, :].astype(BF16), ones_rows], axis=0)


def _moba_kernel(q_ref, k_ref, vt_ref, km_ref, kn_ref, o_ref, qaug_ref, acc_ref, sa_ref, sb_ref, pa_ref, pb_ref):
    kvh = pl.program_id(1)
    i = pl.program_id(2)
    blk = MOBA_BLOCK
    cols = GQA * blk
    nb = km_ref.shape[0]
    scale = HEAD_DIM ** -0.5
    qt = q_ref[...].T * scale
    q_t = jnp.concatenate([qt[g * HEAD_DIM:(g + 1) * HEAD_DIM, :] for g in range(GQA)], axis=1)
    q_hi, q_lo = _split_bf16(q_t)

    km = jnp.where(kvh == 0, km_ref[:, :HEAD_DIM], km_ref[:, HEAD_DIM:])
    km_hi, km_lo = _split_bf16(km)
    gate = _dot(km_hi, q_hi) + _dot(km_hi, q_lo) + _dot(km_lo, q_hi)
    blkid = lax.broadcasted_iota(jnp.int32, (nb, cols), 0).astype(F32)
    past = blkid < i.astype(F32)
    gate = jnp.where(past, gate, NEG)
    sel = jnp.zeros((nb, cols), F32)
    for _ in range(min(MOBA_TOPK, nb)):
        top = jnp.max(gate, axis=0, keepdims=True)
        first = jnp.min(jnp.where(gate == top, blkid, float(nb)), axis=0, keepdims=True)
        hit = blkid == first
        sel = jnp.where(hit & past, 1.0, sel)
        gate = jnp.where(hit, -jnp.inf, gate)
    q2 = (q_t * LOG2E).astype(BF16)
    qaug_ref[0:HEAD_DIM, :] = q2
    pad = qaug_ref.shape[0] - HEAD_DIM - nb
    mask = jnp.where(sel > 0.0, 0.0, NEG)
    if pad:
        mask = jnp.concatenate([mask, jnp.zeros((pad, cols), F32)], axis=0)
    qaug_ref[HEAD_DIM:, :] = mask.astype(BF16)

    def k_rows(j):
        return k_ref[pl.ds(pl.multiple_of(j * blk, blk), blk), :]

    def scores(j):
        return _dot(k_rows(j), qaug_ref[...])

    s = _dot(k_rows(i)[:, :HEAD_DIM], q2)
    kpos = lax.broadcasted_iota(jnp.int32, (blk, cols), 0)
    qpos = lax.broadcasted_iota(jnp.int32, (blk, cols), 1) % blk
    s = jnp.where(kpos <= qpos, s, NEG)
    m0 = jnp.max(s, axis=0, keepdims=True)
    p = jnp.exp2(s - m0)
    acc_ref[...] = _dot(vt_ref[i], p.astype(BF16))

    trips = (i + MOBA_TRIP - 1) // MOBA_TRIP

    def stage(s_ref, j):
        s_j = scores(j)
        s_ref[...] = s_j
        return jnp.max(s_j, axis=0, keepdims=True)

    def absorb(s_ref, s_max, j, m_old):
        m_new = jnp.maximum(m_old, s_max)
        p_j = jnp.exp2(s_ref[...] - m_new)
        corr = jnp.exp2(m_old - m_new)
        acc_ref[...] = acc_ref[...] * corr + _dot(vt_ref[j], p_j.astype(BF16))
        return m_new

    def exact_body(t, carry):
        m, max_cur = carry
        j0 = MOBA_TRIP * t
        bufs = (sa_ref, sb_ref)
        for u in range(MOBA_TRIP):
            max_next = stage(bufs[(u + 1) % 2], jnp.minimum(j0 + u + 1, nb - 1))
            m = absorb(bufs[u % 2], max_cur, j0 + u, m)
            max_cur = max_next
        return m, max_cur

    def stage_early(p_ref, j, r):
        s_j = scores(j)
        p_ref[...] = jnp.exp2(s_j - r).astype(BF16)
        return jnp.max(s_j, axis=0, keepdims=True)

    def absorb_early(p_ref, s_max, r, j, m_old):
        m_new = jnp.maximum(m_old, s_max)
        pv = _dot(vt_ref[j], p_ref[...])
        acc_ref[...] = acc_ref[...] * jnp.exp2(m_old - m_new) + pv * jnp.exp2(r - m_new)
        return m_new

    def early_body(t, carry):
        m, max_cur, r_cur = carry
        j0 = MOBA_TRIP * t
        bufs = (pa_ref, pb_ref)
        for u in range(MOBA_TRIP):
            max_next = stage_early(bufs[(u + 1) % 2], jnp.minimum(j0 + u + 1, nb - 1), m)
            r_next = m
            m = absorb_early(bufs[u % 2], max_cur, r_cur, j0 + u, m)
            max_cur, r_cur = max_next, r_next
        return m, max_cur, r_cur

    kn = jnp.where(kvh == 0, kn_ref[:, :HEAD_DIM], kn_ref[:, HEAD_DIM:])
    kn_max = jnp.max(jnp.max(kn, axis=0, keepdims=True), axis=1, keepdims=True)
    q2f = q2.astype(F32)
    qn_max = jnp.max(jnp.sqrt(jnp.sum(q2f * q2f, axis=0, keepdims=True)), axis=1, keepdims=True)
    headroom = kn_max * qn_max - jnp.min(m0, axis=1, keepdims=True)
    early_ok = headroom[0, 0] < MOBA_EXP_HEADROOM

    @pl.when(early_ok)
    def _():
        lax.fori_loop(0, trips, early_body, (m0, stage_early(pa_ref, 0, m0), m0))

    @pl.when(jnp.logical_not(early_ok))
    def _():
        lax.fori_loop(0, trips, exact_body, (m0, stage(sa_ref, 0)))
    out_t = acc_ref[0:HEAD_DIM, :] / acc_ref[HEAD_DIM:HEAD_DIM + 1, :]
    stacked = jnp.concatenate([out_t[:, g * blk:(g + 1) * blk] for g in range(GQA)], axis=0)
    o_ref[...] = stacked.T.astype(o_ref.dtype)


def _moba(proj):
    bsz, l, _ = proj.shape
    blk = MOBA_BLOCK
    nb = l // blk
    kvw = D_KV * HEAD_DIM
    qgw = GQA * HEAD_DIM
    q0 = C_HEADS * HEAD_DIM + 2 * C_KV * HEAD_DIM
    k0 = q0 + D_HEADS * HEAD_DIM
    qcol0 = q0 // qgw
    nbp = -(-nb // BF16_SUBLANES) * BF16_SUBLANES
    vh = HEAD_DIM + BF16_SUBLANES
    cols = GQA * blk
    kw = HEAD_DIM + nbp
    kmean, knorm, k_aug, vt_bf = pl.pallas_call(
        _moba_prep_kernel,
        grid=(bsz, nb),
        in_specs=[pl.BlockSpec((None, blk, kvw), lambda b, j: (b, j, k0 // kvw)),
                  pl.BlockSpec((None, blk, kvw), lambda b, j: (b, j, k0 // kvw + 1))],
        out_specs=[pl.BlockSpec((None, nb, kvw), lambda b, j: (b, 0, 0)),
                   pl.BlockSpec((None, nb, kvw), lambda b, j: (b, 0, 0)),
                   pl.BlockSpec((None, D_KV, blk, kw), lambda b, j: (b, 0, j, 0)),
                   pl.BlockSpec((None, D_KV, None, vh, blk), lambda b, j: (b, 0, j, 0, 0))],
        out_shape=[jax.ShapeDtypeStruct((bsz, nb, kvw), F32),
                   jax.ShapeDtypeStruct((bsz, nb, kvw), F32),
                   jax.ShapeDtypeStruct((bsz, D_KV, l, kw), BF16),
                   jax.ShapeDtypeStruct((bsz, D_KV, nb, vh, blk), BF16)],
        compiler_params=_cparams("parallel", "arbitrary"),
        name="moba_prep",
    )(proj, proj)
    return pl.pallas_call(
        _moba_kernel,
        grid=(bsz, D_KV, nb),
        in_specs=[pl.BlockSpec((None, blk, qgw), lambda b, h, i: (b, i, qcol0 + h)),
                  pl.BlockSpec((None, None, l, kw), lambda b, h, i: (b, h, 0, 0)),
                  pl.BlockSpec((None, None, nb, vh, blk), lambda b, h, i: (b, h, 0, 0, 0)),
                  pl.BlockSpec((None, nb, kvw), lambda b, h, i: (b, 0, 0)),
                  pl.BlockSpec((None, nb, kvw), lambda b, h, i: (b, 0, 0))],
        out_specs=pl.BlockSpec((None, blk, qgw), lambda b, h, i: (b, i, h)),
        out_shape=jax.ShapeDtypeStruct((bsz, l, D_HEADS * HEAD_DIM), MIX_DTYPE),
        scratch_shapes=[pltpu.VMEM((kw, cols), BF16), pltpu.VMEM((vh, cols), F32),
                        pltpu.VMEM((blk, cols), F32), pltpu.VMEM((blk, cols), F32),
                        pltpu.VMEM((blk, cols), BF16), pltpu.VMEM((blk, cols), BF16)],
        compiler_params=_cparams("parallel", "arbitrary", "arbitrary"),
        name="moba",
    )(proj, k_aug, vt_bf, kmean, knorm)


ROUTER_LANES = 128
MOE_EXPERTS_PER_STEP = 2


def _router_logits(x, wr_ref, br_ref):
    x_hi, x_lo = _split_bf16(x)
    w_hi, w_lo = _split_bf16(wr_ref[...])
    return _dot(x_hi, w_hi) + _dot(x_lo, w_hi) + _dot(x_hi, w_lo) + br_ref[...]


def _lane_ids(shape):
    return lax.broadcasted_iota(jnp.int32, shape, 1).astype(F32)


def _top_group(logits):
    lane = _lane_ids(logits.shape)
    gl = jnp.where(lane < N_GROUPS, logits, -jnp.inf)
    gmax = jnp.max(gl, axis=-1, keepdims=True)
    return jnp.min(jnp.where(gl == gmax, lane, float(ROUTER_LANES)), axis=-1, keepdims=True)


def _moe_gates(logits, g_idx):
    lane = _lane_ids(logits.shape)
    big = float(ROUTER_LANES)
    gl = jnp.where(lane < N_GROUPS, logits, -jnp.inf)
    gmax = jnp.max(gl, axis=-1, keepdims=True)
    g_logit = jnp.sum(jnp.where(lane == g_idx, logits, 0.0), axis=-1, keepdims=True)
    g_top = jnp.exp(g_logit - gmax) / jnp.sum(jnp.exp(gl - gmax), axis=-1, keepdims=True)
    lo = N_GROUPS + EXP_PER_GROUP * g_idx
    el = jnp.where((lane >= lo) & (lane < lo + EXP_PER_GROUP), logits, -jnp.inf)
    m1 = jnp.max(el, axis=-1, keepdims=True)
    i1 = jnp.min(jnp.where(el == m1, lane, big), axis=-1, keepdims=True)
    el2 = jnp.where(lane == i1, -jnp.inf, el)
    m2 = jnp.max(el2, axis=-1, keepdims=True)
    i2 = jnp.min(jnp.where(el2 == m2, lane, big), axis=-1, keepdims=True)
    e21 = jnp.exp(m2 - m1)
    w1 = g_top / (1.0 + e21)
    w2 = w1 * e21
    return jnp.where(lane == i1, w1, 0.0) + jnp.where(lane == i2, w2, 0.0)


def _moe_kernel(x_ref, wr_ref, br_ref, wgu_ref, wd_ref, g_ref, b_ref, o_ref, gates_ref, xb_ref, acc_ref, *, alpha):
    e = pl.program_id(1)

    @pl.when(e == 0)
    def _():
        x = x_ref[...]
        logits = _router_logits(x, wr_ref, br_ref)
        gates_ref[...] = _moe_gates(logits, _top_group(logits))
        xb_ref[...] = x.astype(BF16)
        acc_ref[...] = jnp.zeros_like(acc_ref)

    gates = gates_ref[...]
    lane = lax.broadcasted_iota(jnp.int32, gates.shape, 1)
    xb = xb_ref[...]
    total = None
    for u in range(MOE_EXPERTS_PER_STEP):
        gate_e = jnp.sum(jnp.where(lane == N_GROUPS + e * MOE_EXPERTS_PER_STEP + u, gates, 0.0),
                         axis=-1, keepdims=True)
        hu = _dot(xb, wgu_ref[u].astype(BF16))
        h = _silu(hu[:, :EXP_HIDDEN]) * hu[:, EXP_HIDDEN:] * gate_e
        part = _dot(h.astype(BF16), wd_ref[u].astype(BF16))
        total = part if total is None else total + part
    acc_ref[...] += total

    @pl.when(e == N_EXPERTS // MOE_EXPERTS_PER_STEP - 1)
    def _():
        o_ref[...] = _layer_norm_rows(alpha * x_ref[...] + acc_ref[...], g_ref[...], b_ref[...])


def _moe_ln(x2d, w_group, b_group, w_expert, b_expert, w_gate_up, w_down, g, b, alpha, layer=0, tm=1024):
    t, d = x2d.shape
    w_gate_up = w_gate_up.reshape((-1,) + w_gate_up.shape[-3:])
    w_down = w_down.reshape((-1,) + w_down.shape[-3:])
    eps = MOE_EXPERTS_PER_STEP
    tm = min(tm, t)
    pad = ROUTER_LANES - N_GROUPS - N_EXPERTS
    w_router = jnp.concatenate([w_group, w_expert, jnp.zeros((d, pad), F32)], axis=1).astype(F32)
    b_router = jnp.concatenate([b_group, b_expert, jnp.zeros((pad,), F32)]).astype(F32).reshape(1, ROUTER_LANES)
    return pl.pallas_call(
        functools.partial(_moe_kernel, alpha=alpha),
        grid=(t // tm, N_EXPERTS // eps),
        in_specs=[pl.BlockSpec((tm, d), lambda i, e: (i, 0)),
                  pl.BlockSpec((d, ROUTER_LANES), lambda i, e: (0, 0)),
                  pl.BlockSpec((1, ROUTER_LANES), lambda i, e: (0, 0)),
                  pl.BlockSpec((None, eps, d, 2 * EXP_HIDDEN), lambda i, e: (layer, e, 0, 0)),
                  pl.BlockSpec((None, eps, EXP_HIDDEN, d), lambda i, e: (layer, e, 0, 0)),
                  pl.BlockSpec((1, d), lambda i, e: (0, 0)),
                  pl.BlockSpec((1, d), lambda i, e: (0, 0))],
        out_specs=pl.BlockSpec((tm, d), lambda i, e: (i, 0)),
        out_shape=jax.ShapeDtypeStruct((t, d), F32),
        scratch_shapes=[pltpu.VMEM((tm, ROUTER_LANES), F32), pltpu.VMEM((tm, d), BF16),
                        pltpu.VMEM((tm, d), F32)],
        compiler_params=_cparams("parallel", "arbitrary"),
        name="moe_ln",
    )(x2d, w_router, b_router, w_gate_up, w_down, g.reshape(1, d), b.reshape(1, d))


def kernel(x, hgrn_lb_logits, ev_w_in, ev_a_norm, ev_s5_a_re, ev_s5_a_im, ev_s5_log_dt, ev_s5_b_re, ev_s5_b_im, ev_s5_c_re, ev_s5_c_im, ev_s5_d, ev_s5_w_glu, ev_w_out, od_w_in, od_sinks, od_w_out, ln1_g, ln1_b, moe_w_group, moe_b_group, moe_w_expert, moe_b_expert, moe_w_gate_up, moe_w_down, ln2_g, ln2_b):
    bsz, l, d = x.shape
    depth = ln1_g.shape[0]
    alpha = (2.0 * depth) ** 0.25
    t = bsz * l
    lower_bounds = jnp.cumsum(jax.nn.softmax(hgrn_lb_logits.astype(F32), axis=0), axis=0)
    x2d = x.reshape(t, d)
    for layer in range(depth):
        j = layer // 2
        if layer % 2 == 0:
            proj = _proj(x2d, ev_w_in[j].astype(BF16))
            proj3 = proj.reshape(bsz, l, proj.shape[1])
            ya = _hgrn2(proj3, lower_bounds[layer], ev_a_norm[j])
            y2 = _s5_core(proj, bsz, ev_s5_a_re[j], ev_s5_a_im[j], ev_s5_log_dt[j],
                          ev_s5_b_re[j], ev_s5_b_im[j], ev_s5_c_re[j], ev_s5_c_im[j])
            yb = _s5_glu(y2, proj, ev_s5_d[j], ev_s5_w_glu[j].astype(BF16))
            x2d = _mix_ln(ya.reshape(t, A_W), yb, x2d, ev_w_out[j].astype(BF16), ln1_g[layer], ln1_b[layer], alpha)
        else:
            proj = _proj(x2d, od_w_in[j].astype(BF16))
            proj3 = proj.reshape(bsz, l, proj.shape[1])
            yc = _swa(proj3, od_sinks[j])
            yd = _moba(proj3)
            x2d = _mix_ln(yc.reshape(t, -1), yd.reshape(t, -1), x2d, od_w_out[j].astype(BF16),
                          ln1_g[layer], ln1_b[layer], alpha)
        x2d = _moe_ln(x2d, moe_w_group[layer], moe_b_group[layer], moe_w_expert[layer], moe_b_expert[layer],
                      moe_w_gate_up, moe_w_down, ln2_g[layer], ln2_b[layer], alpha, layer=layer)
    return x2d.reshape(bsz, l, d)
```

```python
import functools
import math

import jax
import jax.numpy as jnp
from jax import lax
from jax.experimental import pallas as pl
from jax.experimental.pallas import tpu as pltpu

F32 = jnp.float32
BF16 = jnp.bfloat16
MIX_DTYPE = BF16
NEG = -1e30
LN_EPS = 1e-5
RMS_EPS = 1e-6

A_HEADS = 4
A_DK = 128
A_DV = 128
A_CHUNK = 64
A_W = A_HEADS * A_DK

B_WIDTH = 512
B_GROUP = 16
B_GROUPS = B_WIDTH // B_GROUP
B_STATE = 64
S5_LC = 16
S5_CW = S5_LC * B_GROUP
S5_SLAB_LANES = 128

HEAD_DIM = 64
C_HEADS = 8
C_KV = 2
WINDOW = 128
SWA_QBLOCK = 256
D_HEADS = 8
D_KV = 2
MOBA_BLOCK = 256
MOBA_TOPK = 3
MOBA_TRIP = 4
MOBA_EXP_HEADROOM = 60.0
LOG2E = math.log2(math.e)
BF16_SUBLANES = 16
GQA = C_HEADS // C_KV

N_GROUPS = 4
EXP_PER_GROUP = 4
N_EXPERTS = 16
EXP_HIDDEN = 256

VMEM_LIMIT_BYTES = 48 * 1024 * 1024


def _cparams(*sem):
    return pltpu.CompilerParams(dimension_semantics=sem, vmem_limit_bytes=VMEM_LIMIT_BYTES)


def _nt_dot(a, b):
    return lax.dot_general(a, b, (((1,), (1,)), ((), ())), preferred_element_type=F32)


def _dot(a, b):
    return jnp.dot(a, b, preferred_element_type=F32)


def _split_bf16(v):
    hi = v.astype(BF16)
    lo = (v - hi.astype(F32)).astype(BF16)
    return hi, lo


def _silu(v):
    return v * jax.nn.sigmoid(v)


def _layer_norm_rows(r, g, b):
    mu = jnp.mean(r, axis=-1, keepdims=True)
    d = r - mu
    var = jnp.mean(d * d, axis=-1, keepdims=True)
    return d * lax.rsqrt(var + LN_EPS) * g + b


def _proj_kernel(x_ref, w_ref, o_ref):
    o_ref[...] = _dot(x_ref[...].astype(BF16), w_ref[...])


def _proj(x2d, w_bf, tm=512):
    t, d = x2d.shape
    n = w_bf.shape[1]
    return pl.pallas_call(
        _proj_kernel,
        grid=(t // tm,),
        in_specs=[pl.BlockSpec((tm, d), lambda i: (i, 0)),
                  pl.BlockSpec((d, n), lambda i: (0, 0))],
        out_specs=pl.BlockSpec((tm, n), lambda i: (i, 0)),
        out_shape=jax.ShapeDtypeStruct((t, n), F32),
        compiler_params=_cparams("parallel"),
        name="in_proj",
    )(x2d, w_bf)


def _mix_ln_kernel(ya_ref, yb_ref, x_ref, w_ref, g_ref, b_ref, o_ref, *, alpha):
    half = ya_ref.shape[1]
    mix = _dot(ya_ref[...].astype(BF16), w_ref[0:half, :]) + _dot(yb_ref[...].astype(BF16), w_ref[half:, :])
    o_ref[...] = _layer_norm_rows(alpha * x_ref[...] + mix, g_ref[...], b_ref[...])


def _mix_ln(ya, yb, x2d, w_out_bf, g, b, alpha, tm=512):
    t, d = x2d.shape
    half = ya.shape[1]
    return pl.pallas_call(
        functools.partial(_mix_ln_kernel, alpha=alpha),
        grid=(t // tm,),
        in_specs=[pl.BlockSpec((tm, half), lambda i: (i, 0)),
                  pl.BlockSpec((tm, half), lambda i: (i, 0)),
                  pl.BlockSpec((tm, d), lambda i: (i, 0)),
                  pl.BlockSpec((2 * half, d), lambda i: (0, 0)),
                  pl.BlockSpec((1, d), lambda i: (0, 0)),
                  pl.BlockSpec((1, d), lambda i: (0, 0))],
        out_specs=pl.BlockSpec((tm, d), lambda i: (i, 0)),
        out_shape=jax.ShapeDtypeStruct((t, d), F32),
        compiler_params=_cparams("parallel"),
        name="mix_ln",
    )(ya, yb, x2d, w_out_bf, g.reshape(1, d), b.reshape(1, d))


def _hgrn2_kernel(proj_ref, lb_ref, ng_ref, tri_ref, o_ref, st_ref, *, tb):
    @pl.when(pl.program_id(1) == 0)
    def _():
        st_ref[...] = jnp.zeros_like(st_ref)

    ql = proj_ref[:, 0:A_W]
    fl = proj_ref[:, A_W:2 * A_W]
    iv = proj_ref[:, 2 * A_W:3 * A_W]
    gt = proj_ref[:, 3 * A_W:4 * A_W]
    lb = lb_ref[...]
    f = lb + (1.0 - lb) * jax.nn.sigmoid(fl)
    k = 1.0 - f
    lf = jnp.log(f)
    lf_hi, lf_lo = _split_bf16(lf)
    tri = tri_ref[...]
    b = _dot(tri, lf_hi) + _dot(tri, lf_lo)
    qd = _silu(ql) * jnp.exp(b)
    ki = k * jnp.exp(-b)
    og = ng_ref[...] * _silu(gt)

    row = lax.broadcasted_iota(jnp.int32, (A_CHUNK, A_CHUNK), 0)
    col = lax.broadcasted_iota(jnp.int32, (A_CHUNK, A_CHUNK), 1)
    causal = col <= row
    v_t = [iv[:, h * A_DV:(h + 1) * A_DV].T for h in range(A_HEADS)]

    for c in range(tb // A_CHUNK):
        r0 = c * A_CHUNK
        b_c = b[r0:r0 + A_CHUNK, :]
        b_last = b[r0 + A_CHUNK - 1:r0 + A_CHUNK, :]
        kt = k[r0:r0 + A_CHUNK, :] * jnp.exp(b_last - b_c)
        dec = jnp.exp(b_last)
        for h in range(A_HEADS):
            ls = slice(h * A_DK, (h + 1) * A_DK)
            qd_h = qd[r0:r0 + A_CHUNK, ls].astype(BF16)
            ki_h = ki[r0:r0 + A_CHUNK, ls].astype(BF16)
            v_h = iv[r0:r0 + A_CHUNK, ls]
            att = jnp.where(causal, _nt_dot(qd_h, ki_h), 0.0)
            s_t = st_ref[h]
            o = _dot(att.astype(BF16), v_h.astype(BF16)) + _nt_dot(qd_h, s_t.astype(BF16))
            st_ref[h] = s_t * dec[:, ls] + _dot(v_t[h][:, r0:r0 + A_CHUNK].astype(BF16), kt[:, ls].astype(BF16))
            o = o * lax.rsqrt(jnp.mean(o * o, axis=-1, keepdims=True) + RMS_EPS)
            o_ref[r0:r0 + A_CHUNK, ls] = (o * og[r0:r0 + A_CHUNK, ls]).astype(o_ref.dtype)


def _hgrn2(proj, lb, norm_g, tb=256):
    bsz, l, _ = proj.shape
    tb = min(tb, l)
    idx = jnp.arange(tb)
    tri = ((idx[:, None] >= idx[None, :]) & (idx[:, None] // A_CHUNK == idx[None, :] // A_CHUNK)).astype(BF16)
    return pl.pallas_call(
        functools.partial(_hgrn2_kernel, tb=tb),
        grid=(bsz, l // tb),
        in_specs=[pl.BlockSpec((None, tb, 4 * A_W), lambda b, t: (b, t, 0)),
                  pl.BlockSpec((1, A_W), lambda b, t: (0, 0)),
                  pl.BlockSpec((1, A_W), lambda b, t: (0, 0)),
                  pl.BlockSpec((tb, tb), lambda b, t: (0, 0))],
        out_specs=pl.BlockSpec((None, tb, A_W), lambda b, t: (b, t, 0)),
        out_shape=jax.ShapeDtypeStruct((bsz, l, A_W), MIX_DTYPE),
        scratch_shapes=[pltpu.VMEM((A_HEADS, A_DV, A_DK), F32)],
        compiler_params=_cparams("parallel", "arbitrary"),
        name="hgrn2",
    )(proj, lb.reshape(1, A_W), norm_g.reshape(1, A_W), tri)


def _s5_weights(a_re, a_im, log_dt, b_re, b_im, c_re, c_im):
    g, p = a_re.shape
    lc = S5_LC
    dt = jnp.exp(log_dt.astype(F32))[:, None]
    ar, ai = a_re.astype(F32), a_im.astype(F32)
    mag = jnp.exp(dt * ar)
    abar_re, abar_im = mag * jnp.cos(dt * ai), mag * jnp.sin(dt * ai)
    den = ar * ar + ai * ai
    xr, xi = abar_re - 1.0, abar_im
    fr = (xr * ar + xi * ai) / den
    fi = (xi * ar - xr * ai) / den
    br, bi = b_re.astype(F32), b_im.astype(F32)
    bb_re = fr[..., None] * br - fi[..., None] * bi
    bb_im = fr[..., None] * bi + fi[..., None] * br
    cr, ci = c_re.astype(F32), c_im.astype(F32)
    tau = jnp.arange(lc + 1, dtype=F32)[:, None, None]
    pmag = jnp.exp(tau * (dt * ar)[None])
    pw_re = pmag * jnp.cos(tau * (dt * ai)[None])
    pw_im = pmag * jnp.sin(tau * (dt * ai)[None])
    hp = lax.Precision.HIGHEST
    ab_re = pw_re[..., None] * bb_re[None] - pw_im[..., None] * bb_im[None]
    ab_im = pw_re[..., None] * bb_im[None] + pw_im[..., None] * bb_re[None]
    kk = (jnp.einsum('gnp,tgpm->tgnm', cr, ab_re, precision=hp)
          - jnp.einsum('gnp,tgpm->tgnm', ci, ab_im, precision=hp))
    s_idx = jnp.arange(lc)[:, None]
    t_idx = jnp.arange(lc)[None, :]
    lag = t_idx - s_idx
    toe = jnp.where((lag >= 0)[:, :, None, None, None], kk[jnp.clip(lag, 0, lc)], 0.0)
    toe = toe.transpose(2, 0, 4, 1, 3).reshape(g, lc * B_GROUP, lc * B_GROUP)
    rev = lc - 1 - jnp.arange(lc)
    wx_re = ab_re[rev].transpose(1, 0, 3, 2).reshape(g, lc * B_GROUP, p)
    wx_im = ab_im[rev].transpose(1, 0, 3, 2).reshape(g, lc * B_GROUP, p)
    pr, pi_ = pw_re[1:], pw_im[1:]
    wy_re = cr[None] * pr[:, :, None, :] - ci[None] * pi_[:, :, None, :]
    wy_im = -(cr[None] * pi_[:, :, None, :] + ci[None] * pr[:, :, None, :])
    wy_re = wy_re.transpose(1, 3, 0, 2).reshape(g, p, lc * B_GROUP)
    wy_im = wy_im.transpose(1, 3, 0, 2).reshape(g, p, lc * B_GROUP)
    return toe, wx_re, wx_im, wy_re, wy_im, pw_re[lc], pw_im[lc]


def _pair_pack_rows(w):
    g, r, c = w.shape
    w = w.reshape(g // 2, 2, r, c)
    z = jnp.zeros((g // 2, r, c), w.dtype)
    top = jnp.concatenate([w[:, 0], z], axis=2)
    bot = jnp.concatenate([z, w[:, 1]], axis=2)
    return jnp.concatenate([top, bot], axis=1)


def _s5_inject_kernel(u_ref, wre_ref, wim_ref, xre_ref, xim_ref):
    lhs = jnp.concatenate([u_ref[0], u_ref[1]], axis=1)
    xre_ref[...] = _dot(lhs, wre_ref[...])
    xim_ref[...] = _dot(lhs, wim_ref[...])


def _s5_scan_kernel(xre_ref, xim_ref, are_ref, aim_ref, hre_ref, him_ref, sre_ref, sim_ref, *, tcs):
    @pl.when(pl.program_id(0) == 0)
    def _():
        sre_ref[...] = jnp.zeros_like(sre_ref)
        sim_ref[...] = jnp.zeros_like(sim_ref)

    ar = are_ref[...]
    ai = aim_ref[...]

    def body(r, carry):
        hr, hi = carry
        hre_ref[r] = hr
        him_ref[r] = hi
        return ar * hr - ai * hi + xre_ref[r], ar * hi + ai * hr + xim_ref[r]

    hr, hi = lax.fori_loop(0, tcs, body, (sre_ref[...], sim_ref[...]), unroll=8)
    sre_ref[...] = hr
    sim_ref[...] = hi


def _s5_readout_kernel(u_ref, toe_ref, wyre_ref, wyim_ref, hre_ref, him_ref, y_ref):
    hre = hre_ref[...].astype(BF16)
    him = him_ref[...].astype(BF16)
    for r in range(2):
        y_ref[r] = (_dot(u_ref[r], toe_ref[r])
                    + _dot(hre, wyre_ref[r]) + _dot(him, wyim_ref[r]))


def _s5_pack_kernel(*refs):
    u_refs, o_ref = refs[:-1], refs[-1]
    tc = o_ref.shape[1]
    gps = S5_SLAB_LANES // B_GROUP
    for r, u_ref in enumerate(u_refs):
        rows = [u_ref[pl.ds(s, tc, stride=S5_LC), :] for s in range(S5_LC)]
        for j in range(gps):
            piece = jnp.concatenate([row[:, j * B_GROUP:(j + 1) * B_GROUP] for row in rows], axis=1)
            o_ref[r * gps + j] = piece.astype(BF16)


def _s5_pack(proj2d, tc=128):
    t, wd = proj2d.shape
    nc = t // S5_LC
    tc = min(tc, nc)
    col0 = (wd - B_WIDTH) // S5_SLAB_LANES
    nslab = B_WIDTH // S5_SLAB_LANES
    return pl.pallas_call(
        _s5_pack_kernel,
        grid=(nc // tc,),
        in_specs=[pl.BlockSpec((tc * S5_LC, S5_SLAB_LANES), functools.partial(lambda i, r: (i, col0 + r), r=r))
                  for r in range(nslab)],
        out_specs=pl.BlockSpec((B_GROUPS, tc, S5_CW), lambda i: (0, i, 0)),
        out_shape=jax.ShapeDtypeStruct((B_GROUPS, nc, S5_CW), BF16),
        compiler_params=_cparams("parallel"),
        name="s5_pack",
    )(*([proj2d] * nslab))


def _s5_core(proj2d, bsz, a_re, a_im, log_dt, b_re, b_im, c_re, c_im):
    l = proj2d.shape[0] // bsz
    g, p, lc, cw = B_GROUPS, B_STATE, S5_LC, S5_CW
    ncb = l // lc
    toe, wx_re, wx_im, wy_re, wy_im, a_re_lc, a_im_lc = _s5_weights(a_re, a_im, log_dt, b_re, b_im, c_re, c_im)
    toe = toe.astype(BF16)
    wx_re_p = _pair_pack_rows(wx_re).astype(BF16)
    wx_im_p = _pair_pack_rows(wx_im).astype(BF16)
    zero = jnp.zeros_like(wy_re)
    even = (jnp.arange(g) % 2 == 0)[:, None, None]
    wy_re_p = jnp.where(even, jnp.concatenate([wy_re, zero], 1), jnp.concatenate([zero, wy_re], 1)).astype(BF16)
    wy_im_p = jnp.where(even, jnp.concatenate([wy_im, zero], 1), jnp.concatenate([zero, wy_im], 1)).astype(BF16)

    u2 = _s5_pack(proj2d)
    tc = min(1024, ncb)
    nct = ncb // tc
    slab = 2 * p
    x_shape = jax.ShapeDtypeStruct((ncb, bsz * g * p), F32)
    x_spec = pl.BlockSpec((tc, slab), lambda j, b, i: (i, b * (g // 2) + j))
    u_spec = pl.BlockSpec((2, tc, cw), lambda j, b, i: (j, b * nct + i, 0))
    xre, xim = pl.pallas_call(
        _s5_inject_kernel,
        grid=(g // 2, bsz, nct),
        in_specs=[u_spec,
                  pl.BlockSpec((None, 2 * cw, slab), lambda j, b, i: (j, 0, 0)),
                  pl.BlockSpec((None, 2 * cw, slab), lambda j, b, i: (j, 0, 0))],
        out_specs=[x_spec, x_spec],
        out_shape=[x_shape, x_shape],
        compiler_params=_cparams("parallel", "parallel", "parallel"),
        name="s5_inject",
    )(u2, wx_re_p, wx_im_p)

    lanes = 512
    rows = bsz * g * p // lanes
    tcs = min(128, ncb)
    a_tile = lambda a: jnp.tile(a.reshape(g * p // lanes, lanes), (bsz, 1))
    seq_spec = pl.BlockSpec((tcs, rows, lanes), lambda i: (i, 0, 0))
    par_spec = pl.BlockSpec((rows, lanes), lambda i: (0, 0))
    seq_shape = jax.ShapeDtypeStruct((ncb, rows, lanes), F32)
    hre, him = pl.pallas_call(
        functools.partial(_s5_scan_kernel, tcs=tcs),
        grid=(ncb // tcs,),
        in_specs=[seq_spec, seq_spec, par_spec, par_spec],
        out_specs=[seq_spec, seq_spec],
        out_shape=[seq_shape, seq_shape],
        scratch_shapes=[pltpu.VMEM((rows, lanes), F32), pltpu.VMEM((rows, lanes), F32)],
        compiler_params=_cparams("arbitrary"),
        name="s5_scan",
    )(xre.reshape(ncb, rows, lanes), xim.reshape(ncb, rows, lanes), a_tile(a_re_lc), a_tile(a_im_lc))
    hre = hre.reshape(ncb, bsz * g * p)
    him = him.reshape(ncb, bsz * g * p)

    y2 = pl.pallas_call(
        _s5_readout_kernel,
        grid=(g // 2, bsz, nct),
        in_specs=[u_spec,
                  pl.BlockSpec((2, cw, cw), lambda j, b, i: (j, 0, 0)),
                  pl.BlockSpec((2, slab, cw), lambda j, b, i: (j, 0, 0)),
                  pl.BlockSpec((2, slab, cw), lambda j, b, i: (j, 0, 0)),
                  x_spec, x_spec],
        out_specs=u_spec,
        out_shape=jax.ShapeDtypeStruct((g, bsz * ncb, cw), F32),
        compiler_params=_cparams("parallel", "parallel", "parallel"),
        name="s5_readout",
    )(u2, toe, wy_re_p, wy_im_p, hre, him)
    return y2


def _s5_glu_kernel(y2_ref, u_ref, d_ref, w_ref, o_ref, yc_ref):
    tc = y2_ref.shape[1]
    gps = S5_SLAB_LANES // B_GROUP
    for r in range(yc_ref.shape[0]):
        grp = [y2_ref[r * gps + j] for j in range(gps)]
        for s in range(S5_LC):
            yc_ref[r, pl.ds(s, tc, stride=S5_LC), :] = jnp.concatenate(
                [blk[:, s * B_GROUP:(s + 1) * B_GROUP] for blk in grp], axis=1)
    yc = jnp.concatenate([yc_ref[r] for r in range(yc_ref.shape[0])], axis=1)
    y = jax.nn.gelu(yc + d_ref[...] * u_ref[...])
    z = _dot(y.astype(BF16), w_ref[...])
    o_ref[...] = (z[:, :B_WIDTH] * jax.nn.sigmoid(z[:, B_WIDTH:])).astype(o_ref.dtype)


def _s5_glu(y2, proj2d, d_skip, w_glu_bf, tc=64):
    t = proj2d.shape[0]
    nc = t // S5_LC
    tc = min(tc, nc)
    tm = tc * S5_LC
    ucol = proj2d.shape[1] // B_WIDTH - 1
    return pl.pallas_call(
        _s5_glu_kernel,
        grid=(nc // tc,),
        in_specs=[pl.BlockSpec((B_GROUPS, tc, S5_CW), lambda i: (0, i, 0)),
                  pl.BlockSpec((tm, B_WIDTH), lambda i: (i, ucol)),
                  pl.BlockSpec((1, B_WIDTH), lambda i: (0, 0)),
                  pl.BlockSpec((B_WIDTH, 2 * B_WIDTH), lambda i: (0, 0))],
        out_specs=pl.BlockSpec((tm, B_WIDTH), lambda i: (i, 0)),
        out_shape=jax.ShapeDtypeStruct((t, B_WIDTH), MIX_DTYPE),
        scratch_shapes=[pltpu.VMEM((B_WIDTH // S5_SLAB_LANES, tm, S5_SLAB_LANES), F32)],
        compiler_params=_cparams("parallel"),
        name="s5_glu",
    )(y2, proj2d, d_skip.reshape(1, B_WIDTH), w_glu_bf)


def _swa_kernel(sink_ref, q_ref, kp_ref, kc_ref, vp_ref, vc_ref, o_ref):
    first = pl.program_id(1) == 0
    w = WINDOW
    qb = q_ref.shape[0]
    nk = w + qb
    cols = GQA * qb
    kpos = lax.broadcasted_iota(jnp.int32, (nk, cols), 0)
    back = lax.broadcasted_iota(jnp.int32, (nk, cols), 1) % qb + w - kpos
    valid = (back >= 0) & (back < w) & jnp.logical_not(first & (kpos < w))
    colhead = lax.broadcasted_iota(jnp.int32, (1, cols), 1) // qb
    qt = q_ref[...].T * (HEAD_DIM ** -0.5 * LOG2E)
    vt = jnp.concatenate([vp_ref[...].T, vc_ref[...].T], axis=1)
    ones_rows = (lax.broadcasted_iota(jnp.int32, (BF16_SUBLANES, nk), 0) == 0).astype(BF16)
    outs = []
    for kv in range(C_KV):
        ls = slice(kv * HEAD_DIM, (kv + 1) * HEAD_DIM)
        k2 = jnp.concatenate([kp_ref[:, ls], kc_ref[:, ls]], axis=0).astype(BF16)
        q_t = jnp.concatenate([qt[(kv * GQA + g) * HEAD_DIM:(kv * GQA + g + 1) * HEAD_DIM, :]
                               for g in range(GQA)], axis=1).astype(BF16)
        s = jnp.where(valid, _dot(k2, q_t), NEG)
        sink = jnp.zeros((1, cols), F32)
        for g in range(GQA):
            sink = jnp.where(colhead == g, sink_ref[kv * GQA + g] * LOG2E, sink)
        m = jnp.maximum(jnp.max(s, axis=0, keepdims=True), sink)
        p = jnp.exp2(s - m)
        vt_aug = jnp.concatenate([vt[ls, :].astype(BF16), ones_rows], axis=0)
        acc = _dot(vt_aug, p.astype(BF16))
        o_t = acc[0:HEAD_DIM, :] / (acc[HEAD_DIM:HEAD_DIM + 1, :] + jnp.exp2(sink - m))
        outs.extend(o_t[:, g * qb:(g + 1) * qb] for g in range(GQA))
    o_ref[...] = jnp.concatenate(outs, axis=0).T.astype(o_ref.dtype)


def _swa(proj, sinks):
    bsz, l, _ = proj.shape
    w = WINDOW
    qw = C_HEADS * HEAD_DIM
    kvw = C_KV * HEAD_DIM
    kcol = qw // kvw
    qb = min(SWA_QBLOCK, l)
    wpb = qb // w
    prev = lambda b, n: (b, jnp.maximum(n * wpb - 1, 0), kcol)
    prev_v = lambda b, n: (b, jnp.maximum(n * wpb - 1, 0), kcol + 1)
    return pl.pallas_call(
        _swa_kernel,
        grid=(bsz, l // qb),
        in_specs=[pl.BlockSpec(memory_space=pltpu.SMEM),
                  pl.BlockSpec((None, qb, qw), lambda b, n: (b, n, 0)),
                  pl.BlockSpec((None, w, kvw), prev),
                  pl.BlockSpec((None, qb, kvw), lambda b, n: (b, n, kcol)),
                  pl.BlockSpec((None, w, kvw), prev_v),
                  pl.BlockSpec((None, qb, kvw), lambda b, n: (b, n, kcol + 1))],
        out_specs=pl.BlockSpec((None, qb, qw), lambda b, n: (b, n, 0)),
        out_shape=jax.ShapeDtypeStruct((bsz, l, qw), MIX_DTYPE),
        compiler_params=_cparams("parallel", "arbitrary"),
        name="swa",
    )(sinks.astype(F32), proj, proj, proj, proj, proj)


def _moba_prep_kernel(k_ref, v_ref, km_ref, kn_ref, ka_ref, vt_ref):
    j = pl.program_id(1)
    k = k_ref[...]
    km_ref[pl.ds(j, 1), :] = jnp.mean(k, axis=0, keepdims=True)
    kr = k.astype(BF16).astype(F32)
    kr2 = kr * kr
    lane = lax.broadcasted_iota(jnp.int32, (1, k.shape[1]), 1)
    norms = [jnp.sqrt(jnp.max(jnp.sum(kr2[:, kv * HEAD_DIM:(kv + 1) * HEAD_DIM], axis=1, keepdims=True),
                              axis=0, keepdims=True)) for kv in range(D_KV)]
    kn_ref[pl.ds(j, 1), :] = jnp.where(lane < HEAD_DIM, norms[0], norms[1])
    nbp = ka_ref.shape[-1] - HEAD_DIM
    onehot = (lax.broadcasted_iota(jnp.int32, (k.shape[0], nbp), 1) == j).astype(BF16)
    vt = v_ref[...].T
    ones_rows = (lax.broadcasted_iota(jnp.int32, (BF16_SUBLANES, k.shape[0]), 0) == 0).astype(BF16)
    for kv in range(D_KV):
        ls = slice(kv * HEAD_DIM, (kv + 1) * HEAD_DIM)
        ka_ref[kv] = jnp.concatenate([k[:, ls].astype(BF16), onehot], axis=1)
        vt_ref[kv] = jnp.concatenate([vt[ls, :].astype(BF16), ones_rows], axis=0)


def _moba_kernel(q_ref, k_ref, vt_ref, km_ref, kn_ref, o_ref, qaug_ref, acc_ref, sa_ref, sb_ref, pa_ref, pb_ref):
    kvh = pl.program_id(1)
    i = pl.program_id(2)
    blk = MOBA_BLOCK
    cols = GQA * blk
    nb = km_ref.shape[0]
    scale = HEAD_DIM ** -0.5
    qt = q_ref[...].T * scale
    q_t = jnp.concatenate([qt[g * HEAD_DIM:(g + 1) * HEAD_DIM, :] for g in range(GQA)], axis=1)
    q_hi, q_lo = _split_bf16(q_t)

    km = jnp.where(kvh == 0, km_ref[:, :HEAD_DIM], km_ref[:, HEAD_DIM:])
    km_hi, km_lo = _split_bf16(km)
    gate = _dot(km_hi, q_hi) + _dot(km_hi, q_lo) + _dot(km_lo, q_hi)
    blkid = lax.broadcasted_iota(jnp.int32, (nb, cols), 0).astype(F32)
    past = blkid < i.astype(F32)
    gate = jnp.where(past, gate, NEG)
    sel = jnp.zeros((nb, cols), F32)
    for _ in range(min(MOBA_TOPK, nb)):
        top = jnp.max(gate, axis=0, keepdims=True)
        first = jnp.min(jnp.where(gate == top, blkid, float(nb)), axis=0, keepdims=True)
        hit = blkid == first
        sel = jnp.where(hit & past, 1.0, sel)
        gate = jnp.where(hit, -jnp.inf, gate)
    q2 = (q_t * LOG2E).astype(BF16)
    qaug_ref[0:HEAD_DIM, :] = q2
    pad = qaug_ref.shape[0] - HEAD_DIM - nb
    mask = jnp.where(sel > 0.0, 0.0, NEG)
    if pad:
        mask = jnp.concatenate([mask, jnp.zeros((pad, cols), F32)], axis=0)
    qaug_ref[HEAD_DIM:, :] = mask.astype(BF16)

    def k_rows(j):
        return k_ref[pl.ds(pl.multiple_of(j * blk, blk), blk), :]

    def scores(j):
        return _dot(k_rows(j), qaug_ref[...])

    s = _dot(k_rows(i)[:, :HEAD_DIM], q2)
    kpos = lax.broadcasted_iota(jnp.int32, (blk, cols), 0)
    qpos = lax.broadcasted_iota(jnp.int32, (blk, cols), 1) % blk
    s = jnp.where(kpos <= qpos, s, NEG)
    m0 = jnp.max(s, axis=0, keepdims=True)
    p = jnp.exp2(s - m0)
    acc_ref[...] = _dot(vt_ref[i], p.astype(BF16))

    trips = (i + MOBA_TRIP - 1) // MOBA_TRIP

    def stage(s_ref, j):
        s_j = scores(j)
        s_ref[...] = s_j
        return jnp.max(s_j, axis=0, keepdims=True)

    def absorb(s_ref, s_max, j, m_old):
        m_new = jnp.maximum(m_old, s_max)
        p_j = jnp.exp2(s_ref[...] - m_new)
        corr = jnp.exp2(m_old - m_new)
        acc_ref[...] = acc_ref[...] * corr + _dot(vt_ref[j], p_j.astype(BF16))
        return m_new

    def exact_body(t, carry):
        m, max_cur = carry
        j0 = MOBA_TRIP * t
        bufs = (sa_ref, sb_ref)
        for u in range(MOBA_TRIP):
            max_next = stage(bufs[(u + 1) % 2], jnp.minimum(j0 + u + 1, nb - 1))
            m = absorb(bufs[u % 2], max_cur, j0 + u, m)
            max_cur = max_next
        return m, max_cur

    def stage_early(p_ref, j, r):
        s_j = scores(j)
        p_ref[...] = jnp.exp2(s_j - r).astype(BF16)
        return jnp.max(s_j, axis=0, keepdims=True)

    def absorb_early(p_ref, s_max, r, j, m_old):
        m_new = jnp.maximum(m_old, s_max)
        pv = _dot(vt_ref[j], p_ref[...])
        acc_ref[...] = acc_ref[...] * jnp.exp2(m_old - m_new) + pv * jnp.exp2(r - m_new)
        return m_new

    def early_body(t, carry):
        m, max_cur, r_cur = carry
        j0 = MOBA_TRIP * t
        bufs = (pa_ref, pb_ref)
        for u in range(MOBA_TRIP):
            max_next = stage_early(bufs[(u + 1) % 2], jnp.minimum(j0 + u + 1, nb - 1), m)
            r_next = m
            m = absorb_early(bufs[u % 2], max_cur, r_cur, j0 + u, m)
            max_cur, r_cur = max_next, r_next
        return m, max_cur, r_cur

    kn = jnp.where(kvh == 0, kn_ref[:, :HEAD_DIM], kn_ref[:, HEAD_DIM:])
    kn_max = jnp.max(jnp.max(kn, axis=0, keepdims=True), axis=1, keepdims=True)
    q2f = q2.astype(F32)
    qn_max = jnp.max(jnp.sqrt(jnp.sum(q2f * q2f, axis=0, keepdims=True)), axis=1, keepdims=True)
    headroom = kn_max * qn_max - jnp.min(m0, axis=1, keepdims=True)
    early_ok = headroom[0, 0] < MOBA_EXP_HEADROOM

    @pl.when(early_ok)
    def _():
        lax.fori_loop(0, trips, early_body, (m0, stage_early(pa_ref, 0, m0), m0))

    @pl.when(jnp.logical_not(early_ok))
    def _():
        lax.fori_loop(0, trips, exact_body, (m0, stage(sa_ref, 0)))
    out_t = acc_ref[0:HEAD_DIM, :] / acc_ref[HEAD_DIM:HEAD_DIM + 1, :]
    stacked = jnp.concatenate([out_t[:, g * blk:(g + 1) * blk] for g in range(GQA)], axis=0)
    o_ref[...] = stacked.T.astype(o_ref.dtype)


def _moba(proj):
    bsz, l, _ = proj.shape
    blk = MOBA_BLOCK
    nb = l // blk
    kvw = D_KV * HEAD_DIM
    qgw = GQA * HEAD_DIM
    q0 = C_HEADS * HEAD_DIM + 2 * C_KV * HEAD_DIM
    k0 = q0 + D_HEADS * HEAD_DIM
    qcol0 = q0 // qgw
    nbp = -(-nb // BF16_SUBLANES) * BF16_SUBLANES
    vh = HEAD_DIM + BF16_SUBLANES
    cols = GQA * blk
    kw = HEAD_DIM + nbp
    kmean, knorm, k_aug, vt_bf = pl.pallas_call(
        _moba_prep_kernel,
        grid=(bsz, nb),
        in_specs=[pl.BlockSpec((None, blk, kvw), lambda b, j: (b, j, k0 // kvw)),
                  pl.BlockSpec((None, blk, kvw), lambda b, j: (b, j, k0 // kvw + 1))],
        out_specs=[pl.BlockSpec((None, nb, kvw), lambda b, j: (b, 0, 0)),
                   pl.BlockSpec((None, nb, kvw), lambda b, j: (b, 0, 0)),
                   pl.BlockSpec((None, D_KV, blk, kw), lambda b, j: (b, 0, j, 0)),
                   pl.BlockSpec((None, D_KV, None, vh, blk), lambda b, j: (b, 0, j, 0, 0))],
        out_shape=[jax.ShapeDtypeStruct((bsz, nb, kvw), F32),
                   jax.ShapeDtypeStruct((bsz, nb, kvw), F32),
                   jax.ShapeDtypeStruct((bsz, D_KV, l, kw), BF16),
                   jax.ShapeDtypeStruct((bsz, D_KV, nb, vh, blk), BF16)],
        compiler_params=_cparams("parallel", "arbitrary"),
        name="moba_prep",
    )(proj, proj)
    return pl.pallas_call(
        _moba_kernel,
        grid=(bsz, D_KV, nb),
        in_specs=[pl.BlockSpec((None, blk, qgw), lambda b, h, i: (b, i, qcol0 + h)),
                  pl.BlockSpec((None, None, l, kw), lambda b, h, i: (b, h, 0, 0)),
                  pl.BlockSpec((None, None, nb, vh, blk), lambda b, h, i: (b, h, 0, 0, 0)),
                  pl.BlockSpec((None, nb, kvw), lambda b, h, i: (b, 0, 0)),
                  pl.BlockSpec((None, nb, kvw), lambda b, h, i: (b, 0, 0))],
        out_specs=pl.BlockSpec((None, blk, qgw), lambda b, h, i: (b, i, h)),
        out_shape=jax.ShapeDtypeStruct((bsz, l, D_HEADS * HEAD_DIM), MIX_DTYPE),
        scratch_shapes=[pltpu.VMEM((kw, cols), BF16), pltpu.VMEM((vh, cols), F32),
                        pltpu.VMEM((blk, cols), F32), pltpu.VMEM((blk, cols), F32),
                        pltpu.VMEM((blk, cols), BF16), pltpu.VMEM((blk, cols), BF16)],
        compiler_params=_cparams("parallel", "arbitrary", "arbitrary"),
        name="moba",
    )(proj, k_aug, vt_bf, kmean, knorm)


ROUTER_LANES = 128
MOE_EXPERTS_PER_STEP = 2


def _router_logits(x, wr_ref, br_ref):
    x_hi, x_lo = _split_bf16(x)
    w_hi, w_lo = _split_bf16(wr_ref[...])
    return _dot(x_hi, w_hi) + _dot(x_lo, w_hi) + _dot(x_hi, w_lo) + br_ref[...]


def _lane_ids(shape):
    return lax.broadcasted_iota(jnp.int32, shape, 1).astype(F32)


def _top_group(logits):
    lane = _lane_ids(logits.shape)
    gl = jnp.where(lane < N_GROUPS, logits, -jnp.inf)
    gmax = jnp.max(gl, axis=-1, keepdims=True)
    return jnp.min(jnp.where(gl == gmax, lane, float(ROUTER_LANES)), axis=-1, keepdims=True)


def _moe_gates(logits, g_idx):
    lane = _lane_ids(logits.shape)
    big = float(ROUTER_LANES)
    gl = jnp.where(lane < N_GROUPS, logits, -jnp.inf)
    gmax = jnp.max(gl, axis=-1, keepdims=True)
    g_logit = jnp.sum(jnp.where(lane == g_idx, logits, 0.0), axis=-1, keepdims=True)
    g_top = jnp.exp(g_logit - gmax) / jnp.sum(jnp.exp(gl - gmax), axis=-1, keepdims=True)
    lo = N_GROUPS + EXP_PER_GROUP * g_idx
    el = jnp.where((lane >= lo) & (lane < lo + EXP_PER_GROUP), logits, -jnp.inf)
    m1 = jnp.max(el, axis=-1, keepdims=True)
    i1 = jnp.min(jnp.where(el == m1, lane, big), axis=-1, keepdims=True)
    el2 = jnp.where(lane == i1, -jnp.inf, el)
    m2 = jnp.max(el2, axis=-1, keepdims=True)
    i2 = jnp.min(jnp.where(el2 == m2, lane, big), axis=-1, keepdims=True)
    e21 = jnp.exp(m2 - m1)
    w1 = g_top / (1.0 + e21)
    w2 = w1 * e21
    return jnp.where(lane == i1, w1, 0.0) + jnp.where(lane == i2, w2, 0.0)


def _moe_kernel(x_ref, wr_ref, br_ref, wgu_ref, wd_ref, g_ref, b_ref, o_ref, gates_ref, xb_ref, acc_ref, *, alpha):
    e = pl.program_id(1)

    @pl.when(e == 0)
    def _():
        x = x_ref[...]
        logits = _router_logits(x, wr_ref, br_ref)
        gates_ref[...] = _moe_gates(logits, _top_group(logits))
        xb_ref[...] = x.astype(BF16)
        acc_ref[...] = jnp.zeros_like(acc_ref)

    gates = gates_ref[...]
    lane = lax.broadcasted_iota(jnp.int32, gates.shape, 1)
    xb = xb_ref[...]
    total = None
    for u in range(MOE_EXPERTS_PER_STEP):
        gate_e = jnp.sum(jnp.where(lane == N_GROUPS + e * MOE_EXPERTS_PER_STEP + u, gates, 0.0),
                         axis=-1, keepdims=True)
        hu = _dot(xb, wgu_ref[u].astype(BF16))
        h = _silu(hu[:, :EXP_HIDDEN]) * hu[:, EXP_HIDDEN:] * gate_e
        part = _dot(h.astype(BF16), wd_ref[u].astype(BF16))
        total = part if total is None else total + part
    acc_ref[...] += total

    @pl.when(e == N_EXPERTS // MOE_EXPERTS_PER_STEP - 1)
    def _():
        o_ref[...] = _layer_norm_rows(alpha * x_ref[...] + acc_ref[...], g_ref[...], b_ref[...])


def _moe_ln(x2d, w_group, b_group, w_expert, b_expert, w_gate_up, w_down, g, b, alpha, layer=0, tm=1024):
    t, d = x2d.shape
    w_gate_up = w_gate_up.reshape((-1,) + w_gate_up.shape[-3:])
    w_down = w_down.reshape((-1,) + w_down.shape[-3:])
    eps = MOE_EXPERTS_PER_STEP
    tm = min(tm, t)
    pad = ROUTER_LANES - N_GROUPS - N_EXPERTS
    w_router = jnp.concatenate([w_group, w_expert, jnp.zeros((d, pad), F32)], axis=1).astype(F32)
    b_router = jnp.concatenate([b_group, b_expert, jnp.zeros((pad,), F32)]).astype(F32).reshape(1, ROUTER_LANES)
    return pl.pallas_call(
        functools.partial(_moe_kernel, alpha=alpha),
        grid=(t // tm, N_EXPERTS // eps),
        in_specs=[pl.BlockSpec((tm, d), lambda i, e: (i, 0)),
                  pl.BlockSpec((d, ROUTER_LANES), lambda i, e: (0, 0)),
                  pl.BlockSpec((1, ROUTER_LANES), lambda i, e: (0, 0)),
                  pl.BlockSpec((None, eps, d, 2 * EXP_HIDDEN), lambda i, e: (layer, e, 0, 0)),
                  pl.BlockSpec((None, eps, EXP_HIDDEN, d), lambda i, e: (layer, e, 0, 0)),
                  pl.BlockSpec((1, d), lambda i, e: (0, 0)),
                  pl.BlockSpec((1, d), lambda i, e: (0, 0))],
        out_specs=pl.BlockSpec((tm, d), lambda i, e: (i, 0)),
        out_shape=jax.ShapeDtypeStruct((t, d), F32),
        scratch_shapes=[pltpu.VMEM((tm, ROUTER_LANES), F32), pltpu.VMEM((tm, d), BF16),
                        pltpu.VMEM((tm, d), F32)],
        compiler_params=_cparams("parallel", "arbitrary"),
        name="moe_ln",
    )(x2d, w_router, b_router, w_gate_up, w_down, g.reshape(1, d), b.reshape(1, d))


def kernel(x, hgrn_lb_logits, ev_w_in, ev_a_norm, ev_s5_a_re, ev_s5_a_im, ev_s5_log_dt, ev_s5_b_re, ev_s5_b_im, ev_s5_c_re, ev_s5_c_im, ev_s5_d, ev_s5_w_glu, ev_w_out, od_w_in, od_sinks, od_w_out, ln1_g, ln1_b, moe_w_group, moe_b_group, moe_w_expert, moe_b_expert, moe_w_gate_up, moe_w_down, ln2_g, ln2_b):
    bsz, l, d = x.shape
    depth = ln1_g.shape[0]
    alpha = (2.0 * depth) ** 0.25
    t = bsz * l
    lower_bounds = jnp.cumsum(jax.nn.softmax(hgrn_lb_logits.astype(F32), axis=0), axis=0)
    x2d = x.reshape(t, d)
    for layer in range(depth):
        j = layer // 2
        if layer % 2 == 0:
            proj = _proj(x2d, ev_w_in[j].astype(BF16))
            proj3 = proj.reshape(bsz, l, proj.shape[1])
            ya = _hgrn2(proj3, lower_bounds[layer], ev_a_norm[j])
            y2 = _s5_core(proj, bsz, ev_s5_a_re[j], ev_s5_a_im[j], ev_s5_log_dt[j],
                          ev_s5_b_re[j], ev_s5_b_im[j], ev_s5_c_re[j], ev_s5_c_im[j])
            yb = _s5_glu(y2, proj, ev_s5_d[j], ev_s5_w_glu[j].astype(BF16))
            x2d = _mix_ln(ya.reshape(t, A_W), yb, x2d, ev_w_out[j].astype(BF16), ln1_g[layer], ln1_b[layer], alpha)
        else:
            proj = _proj(x2d, od_w_in[j].astype(BF16))
            proj3 = proj.reshape(bsz, l, proj.shape[1])
            yc = _swa(proj3, od_sinks[j])
            yd = _moba(proj3)
            x2d = _mix_ln(yc.reshape(t, -1), yd.reshape(t, -1), x2d, od_w_out[j].astype(BF16),
                          ln1_g[layer], ln1_b[layer], alpha)
        x2d = _moe_ln(x2d, moe_w_group[layer], moe_b_group[layer], moe_w_expert[layer], moe_b_expert[layer],
                      moe_w_gate_up, moe_w_down, ln2_g[layer], ln2_b[layer], alpha, layer=layer)
    return x2d.reshape(bsz, l, d)
```
